```python
import jax, jax.numpy as jnp
from jax import lax
import numpy as np

D_MODEL = 1024
BATCH = 1
SEQ = 16384
DEPTH = 2

N_A_LAYERS = DEPTH // 2
N_B_LAYERS = DEPTH - N_A_LAYERS

CONV_WIDTH = 31
CONV_CH = D_MODEL

HEAD_DIM = 64
HEADS_PER_GROUP = D_MODEL // HEAD_DIM
ATTN_WIDTH = HEADS_PER_GROUP * HEAD_DIM
DILATED_GROUPS = ((128, 1), (512, 4), (2048, 16))
N_GROUPS = len(DILATED_GROUPS)
Q_WIDTH = N_GROUPS * ATTN_WIDTH
BLOCK = 128
ALIBI_MAX_EXP = 8.0

ALPHA = (2.0 * DEPTH) ** 0.25
BETA = (8.0 * DEPTH) ** -0.25
LN_EPS = 1e-5

kernel_name = "yoco_conformer_conv_dilated_attn_deepnorm"


def layer_norm(x, g, b):
    xf = x.astype(jnp.float32)
    mu = jnp.mean(xf, axis=-1, keepdims=True)
    xc = xf - mu
    var = jnp.mean(xc * xc, axis=-1, keepdims=True)
    y = xc * lax.rsqrt(var + LN_EPS) * g.astype(jnp.float32) + b.astype(jnp.float32)
    return y.astype(x.dtype)


def alibi_slopes(n_heads):
    h = jnp.arange(1, n_heads + 1, dtype=jnp.float32)
    return jnp.exp2(-ALIBI_MAX_EXP * h / n_heads)


def conformer_conv_branch(x, w_in, b_in, w_dw, b_dw, ln_g, ln_b, w_out, b_out):
    h = x @ w_in + b_in
    a, a_gate, z = jnp.split(h, 3, axis=-1)
    u = a * jax.nn.sigmoid(a_gate)
    u = lax.conv_general_dilated(
        u, w_dw[:, None, :].astype(u.dtype), window_strides=(1,),
        padding=((CONV_WIDTH - 1, 0),),
        dimension_numbers=('NWC', 'WIO', 'NWC'),
        feature_group_count=CONV_CH) + b_dw
    u = jax.nn.silu(layer_norm(u, ln_g, ln_b))
    return (u * jax.nn.silu(z)) @ w_out + b_out


def dilated_window_attention(q, k, v, slopes, window, dilation):
    B, S, H, hd = q.shape
    n_back = window // dilation
    span = dilation * BLOCK
    s_pad = -(-S // span) * span
    L = s_pad // dilation
    nb = L // BLOCK
    pad = ((0, 0), (0, s_pad - S), (0, 0), (0, 0))

    def to_blocks(t):
        t = jnp.pad(t.astype(jnp.float32), pad).reshape(B, L, dilation, H, hd)
        return t.transpose(0, 2, 1, 3, 4).reshape(B, dilation, nb, BLOCK, H, hd)

    def with_prev(t):
        prev = jnp.pad(t[:, :, :-1], ((0, 0), (0, 0), (1, 0), (0, 0), (0, 0), (0, 0)))
        return jnp.concatenate([prev, t], axis=3)

    qb = to_blocks(q)
    kk = with_prev(to_blocks(k))
    vv = with_prev(to_blocks(v))

    scores = jnp.einsum('brnqhd,brnkhd->brnhqk', qb, kk) * (hd ** -0.5)
    qi = jnp.arange(BLOCK)[:, None]
    kj = jnp.arange(2 * BLOCK)[None, :]
    dist = qi + BLOCK - kj
    band = (dist >= 0) & (dist <= n_back)
    first = (jnp.arange(nb) == 0)[:, None, None]
    valid = band[None] & ~(first & (kj < BLOCK)[None])
    bias = -slopes[:, None, None] * (dilation * dist).astype(jnp.float32)[None]
    scores = jnp.where(valid[None, None, :, None], scores + bias, -jnp.inf)

    m = jnp.max(scores, axis=-1, keepdims=True)
    p = jnp.exp(scores - m)
    denom = jnp.sum(p, axis=-1, keepdims=True)
    o = jnp.einsum('brnhqk,brnkhd->brnqhd', p / denom, vv)
    lse = (m + jnp.log(denom))[..., 0]

    o = o.reshape(B, dilation, L, H, hd).transpose(0, 2, 1, 3, 4).reshape(B, s_pad, H, hd)[:, :S]
    lse = lse.transpose(0, 1, 2, 4, 3).reshape(B, dilation, L, H).transpose(0, 2, 1, 3)
    lse = lse.reshape(B, s_pad, H)[:, :S]
    return o, lse


def dilated_attention_branch(x, w_in, w_out, b_out, k_shared, v_shared):
    B, S, _ = x.shape
    h = x @ w_in
    q = h[..., :Q_WIDTH].reshape(B, S, N_GROUPS, HEADS_PER_GROUP, HEAD_DIM)
    z = h[..., Q_WIDTH:]
    slopes = alibi_slopes(HEADS_PER_GROUP)
    outs, lses = [], []
    for g, (window, dilation) in enumerate(DILATED_GROUPS):
        o, l = dilated_window_attention(q[:, :, g], k_shared[:, :, g], v_shared[:, :, g],
                                        slopes, window, dilation)
        outs.append(o)
        lses.append(l)
    wts = jax.nn.softmax(jnp.stack(lses, axis=0), axis=0)
    o = jnp.sum(wts[..., None] * jnp.stack(outs, axis=0), axis=0)
    o = o.reshape(B, S, ATTN_WIDTH).astype(x.dtype)
    return (o * jax.nn.silu(z)) @ w_out + b_out


def setup_inputs(seed: int = 0) -> dict:
    key = jax.random.key(seed)
    ks = jax.random.split(key, 16)
    f32 = jnp.float32
    nrm = lambda k, shape: jax.random.normal(k, shape, dtype=f32)
    C, D = CONV_CH, D_MODEL
    return {
        "x": nrm(ks[0], (BATCH, SEQ, D)),
        "a_w_in": nrm(ks[1], (N_A_LAYERS, D, 3 * C)) * D ** -0.5,
        "a_b_in": 0.02 * nrm(ks[2], (N_A_LAYERS, 3 * C)),
        "a_w_dw": nrm(ks[3], (N_A_LAYERS, CONV_WIDTH, C)) * CONV_WIDTH ** -0.5,
        "a_b_dw": 0.02 * nrm(ks[4], (N_A_LAYERS, C)),
        "a_ln_g": 1.0 + 0.02 * nrm(ks[5], (N_A_LAYERS, C)),
        "a_ln_b": 0.02 * nrm(ks[6], (N_A_LAYERS, C)),
        "a_w_out": nrm(ks[7], (N_A_LAYERS, C, D)) * (C ** -0.5 * BETA),
        "a_b_out": 0.02 * nrm(ks[8], (N_A_LAYERS, D)),
        "kv_w": nrm(ks[9], (D, 2 * Q_WIDTH)) * D ** -0.5,
        "b_w_in": nrm(ks[10], (N_B_LAYERS, D, Q_WIDTH + ATTN_WIDTH)) * D ** -0.5,
        "b_w_out": nrm(ks[11], (N_B_LAYERS, ATTN_WIDTH, D)) * (ATTN_WIDTH ** -0.5 * BETA),
        "b_b_out": 0.02 * nrm(ks[12], (N_B_LAYERS, D)),
        "post_ln_g": 1.0 + 0.02 * nrm(ks[13], (DEPTH, D)),
        "post_ln_b": 0.02 * nrm(ks[14], (DEPTH, D)),
    }


def reference(x, a_w_in, a_b_in, a_w_dw, a_b_dw, a_ln_g, a_ln_b, a_w_out, a_b_out,
              kv_w, b_w_in, b_w_out, b_b_out, post_ln_g, post_ln_b):
    B, S, _ = x.shape
    k_shared = v_shared = None
    for layer in range(DEPTH):
        if layer < N_A_LAYERS:
            i = layer
            y = conformer_conv_branch(x, a_w_in[i], a_b_in[i], a_w_dw[i], a_b_dw[i],
                                      a_ln_g[i], a_ln_b[i], a_w_out[i], a_b_out[i])
        else:
            if layer == N_A_LAYERS:
                kv = (x @ kv_w).reshape(B, S, 2, N_GROUPS, HEADS_PER_GROUP, HEAD_DIM)
                k_shared, v_shared = kv[:, :, 0], kv[:, :, 1]
            i = layer - N_A_LAYERS
            y = dilated_attention_branch(x, b_w_in[i], b_w_out[i], b_b_out[i], k_shared, v_shared)
        x = layer_norm(ALPHA * x + y, post_ln_g[layer], post_ln_b[layer])
    return x
```

```python
import functools

import jax
import jax.numpy as jnp
from jax import lax
from jax.experimental import pallas as pl
from jax.experimental.pallas import tpu as pltpu

F32 = jnp.float32
BF16 = jnp.bfloat16

D_MODEL = 1024
CONV_WIDTH = 31
HEAD_DIM = 64
N_HEADS = 16
N_GROUPS = 3
DILATED_GROUPS = ((128, 1), (512, 4), (2048, 16))
BLOCK = 128
ALIBI_MAX_EXP = 8.0
DEPTH = 2
ALPHA = (2.0 * DEPTH) ** 0.25
LN_EPS = 1e-5
NEG_BIG = -1e30

LANES = 128
N_SLABS = D_MODEL // LANES
HALO = 32
CONV_ROWS = 64
VMEM_LIMIT = 56 * 1024 * 1024

TM_CONV = 256
TM_PROJ = 512
TQ_ATTN = 512
TM_OUT = 512


def _layer_norm(x, g, b):
    mu = jnp.mean(x, axis=-1, keepdims=True)
    xc = x - mu
    var = jnp.mean(xc * xc, axis=-1, keepdims=True)
    return xc * lax.rsqrt(var + LN_EPS) * g + b


def _silu(x):
    return x * jax.nn.sigmoid(x)


def _conv_layer_kernel(x_ref, w_in_ref, b_in_ref, w_dw_ref, b_dw_ref, lng_ref, lnb_ref,
                       w_out_ref, b_out_ref, pg_ref, pb_ref,
                       x1_ref, x1b_ref, ubuf, cbuf):
    tm = x_ref.shape[0]

    @pl.when(pl.program_id(0) == 0)
    def _():
        ubuf[:, 0:HALO, :] = jnp.zeros((N_SLABS, HALO, LANES), F32)

    x = x_ref[...]
    h = jnp.dot(x.astype(BF16), w_in_ref[...], preferred_element_type=F32) + b_in_ref[...]
    for c in range(N_SLABS):
        a = h[:, c * LANES:(c + 1) * LANES]
        gate = h[:, D_MODEL + c * LANES:D_MODEL + (c + 1) * LANES]
        ubuf[c, HALO:HALO + tm, :] = a * jax.nn.sigmoid(gate)

    def conv_chunk(idx, carry):
        c = idx // (tm // CONV_ROWS)
        r0 = pl.multiple_of((idx % (tm // CONV_ROWS)) * CONV_ROWS, CONV_ROWS)
        acc = jnp.zeros((CONV_ROWS, LANES), F32)
        for j in range(CONV_WIDTH):
            off = HALO - (CONV_WIDTH - 1) + j
            acc = acc + w_dw_ref[c, j:j + 1, :] * ubuf[c, pl.ds(r0 + off, CONV_ROWS), :]
        cbuf[c, pl.ds(r0, CONV_ROWS), :] = acc
        return carry

    lax.fori_loop(0, N_SLABS * (tm // CONV_ROWS), conv_chunk, 0)
    ubuf[:, 0:HALO, :] = ubuf[:, tm:tm + HALO, :]

    u = jnp.concatenate([cbuf[c] for c in range(N_SLABS)], axis=1) + b_dw_ref[...]
    u = _silu(_layer_norm(u, lng_ref[...], lnb_ref[...]))
    z = h[:, 2 * D_MODEL:]
    gated = (u * _silu(z)).astype(BF16)
    y = jnp.dot(gated, w_out_ref[...], preferred_element_type=F32) + b_out_ref[...]
    x1 = _layer_norm(ALPHA * x + y, pg_ref[...], pb_ref[...])
    x1_ref[...] = x1
    x1b_ref[...] = x1.astype(BF16)


def _conv_layer(x, w_in, b_in, w_dw, b_dw, ln_g, ln_b, w_out, b_out, pg, pb):
    s = x.shape[0]
    tm = TM_CONV
    full = lambda shape: pl.BlockSpec(shape, lambda i: (0,) * len(shape))
    row = pl.BlockSpec((tm, D_MODEL), lambda i: (i, 0))
    return pl.pallas_call(
        _conv_layer_kernel,
        grid=(s // tm,),
        in_specs=[row, full((D_MODEL, 3 * D_MODEL)), full((1, 3 * D_MODEL)),
                  full((N_SLABS, CONV_WIDTH, LANES)), full((1, D_MODEL)), full((1, D_MODEL)),
                  full((1, D_MODEL)), full((D_MODEL, D_MODEL)), full((1, D_MODEL)),
                  full((1, D_MODEL)), full((1, D_MODEL))],
        out_specs=[row, row],
        out_shape=[jax.ShapeDtypeStruct((s, D_MODEL), F32),
                   jax.ShapeDtypeStruct((s, D_MODEL), BF16)],
        scratch_shapes=[pltpu.VMEM((N_SLABS, HALO + tm, LANES), F32),
                        pltpu.VMEM((N_SLABS, tm, LANES), F32)],
        compiler_params=pltpu.CompilerParams(dimension_semantics=("arbitrary",),
                                             vmem_limit_bytes=VMEM_LIMIT),
        name="conv_layer",
    )(x, w_in, b_in, w_dw, b_dw, ln_g, ln_b, w_out, b_out, pg, pb)


def _qkv_kernel(x_ref, w_ref, q_ref, k_ref, v_ref):
    h = jnp.dot(x_ref[...], w_ref[...], preferred_element_type=F32)
    q_ref[...] = (h[:, :D_MODEL] * (HEAD_DIM ** -0.5)).astype(BF16)
    k_ref[...] = h[:, D_MODEL:2 * D_MODEL].astype(BF16)
    v_ref[...] = h[:, 2 * D_MODEL:].astype(BF16)


def _qkv_proj(x1b, w_qkv, dilation):
    s = x1b.shape[0]
    tm = TM_PROJ
    nj = s // dilation // tm
    x_view = x1b.reshape(s // dilation, dilation * D_MODEL)
    out = pl.BlockSpec((tm, D_MODEL), lambda i: (i, 0))
    shape = jax.ShapeDtypeStruct((s, D_MODEL), BF16)
    return pl.pallas_call(
        _qkv_kernel,
        grid=(s // tm,),
        in_specs=[pl.BlockSpec((tm, D_MODEL), lambda i: (i % nj, i // nj)),
                  pl.BlockSpec((D_MODEL, 3 * D_MODEL), lambda i: (0, 0))],
        out_specs=[out, out, out],
        out_shape=[shape, shape, shape],
        compiler_params=pltpu.CompilerParams(dimension_semantics=("arbitrary",),
                                             vmem_limit_bytes=VMEM_LIMIT),
        name=f"qkv_proj_d{dilation}",
    )(x_view, w_qkv)


def _attn_kernel(q_ref, kp_ref, kc_ref, vp_ref, vc_ref, bias_ref, o_ref, lse_ref,
                 kbuf, vbuf, *, tiles_per_residue):
    tq = q_ref.shape[0]
    first_tile = (pl.program_id(0) % tiles_per_residue) == 0
    kbuf[0:BLOCK, :] = kp_ref[...]
    kbuf[BLOCK:, :] = kc_ref[...]
    vbuf[0:BLOCK, :] = vp_ref[...]
    vbuf[BLOCK:, :] = vc_ref[...]
    lane = lax.broadcasted_iota(jnp.int32, (1, LANES), 1)
    low = lane < HEAD_DIM
    nt = (((1,), (1,)), ((), ()))

    def block(b, carry):
        r0 = pl.multiple_of(b * BLOCK, BLOCK)
        variant = jnp.where(jnp.logical_and(first_tile, b == 0), 1, 0)
        lse_tile = jnp.zeros((BLOCK, LANES), F32)
        for hp in range(N_HEADS // 2):
            cs = slice(hp * LANES, (hp + 1) * LANES)
            q2 = q_ref[pl.ds(r0, BLOCK), cs]
            k2 = kbuf[pl.ds(r0, 2 * BLOCK), cs]
            v2 = vbuf[pl.ds(r0, 2 * BLOCK), cs]
            zk = jnp.zeros_like(k2)
            ps, inv = [], []
            for half in range(2):
                head = 2 * hp + half
                keep = low if half == 0 else jnp.logical_not(low)
                s = lax.dot_general(q2, jnp.where(keep, k2, zk), nt,
                                    preferred_element_type=F32) + bias_ref[variant, head]
                m = jnp.max(s, axis=-1, keepdims=True)
                p = jnp.exp(s - m)
                den = jnp.sum(p, axis=-1, keepdims=True)
                ps.append(p.astype(BF16))
                inv.append(1.0 / den)
                lse_tile = jnp.where(lane == head, m + jnp.log(den), lse_tile)
            pcat = jnp.concatenate(ps, axis=1)
            vstack = jnp.concatenate([jnp.where(low, v2, zk), jnp.where(low, zk, v2)], axis=0)
            o2 = jnp.dot(pcat, vstack, preferred_element_type=F32)
            o_ref[pl.ds(r0, BLOCK), cs] = o2 * jnp.where(low, inv[0], inv[1])
        lse_ref[pl.ds(r0, BLOCK), :] = lse_tile
        return carry

    lax.fori_loop(0, tq // BLOCK, block, 0)


def _attention_group(q, k, v, bias, dilation):
    s = q.shape[0]
    tq = TQ_ATTN
    nj = s // dilation // tq
    bpt = tq // BLOCK
    cur = pl.BlockSpec((tq, D_MODEL), lambda i: (i, 0))
    prev = pl.BlockSpec((BLOCK, D_MODEL), lambda i: (jnp.maximum(i * bpt - 1, 0), 0))
    o, lse = pl.pallas_call(
        functools.partial(_attn_kernel, tiles_per_residue=nj),
        grid=(s // tq,),
        in_specs=[cur, prev, cur, prev, cur,
                  pl.BlockSpec((2, N_HEADS, BLOCK, 2 * BLOCK), lambda i: (0, 0, 0, 0))],
        out_specs=[pl.BlockSpec((tq, D_MODEL), lambda i: (i % nj, i // nj)),
                   pl.BlockSpec((tq, LANES), lambda i: (i % nj, i // nj))],
        out_shape=[jax.ShapeDtypeStruct((s // dilation, dilation * D_MODEL), F32),
                   jax.ShapeDtypeStruct((s // dilation, dilation * LANES), F32)],
        scratch_shapes=[pltpu.VMEM((BLOCK + tq, D_MODEL), BF16),
                        pltpu.VMEM((BLOCK + tq, D_MODEL), BF16)],
        compiler_params=pltpu.CompilerParams(dimension_semantics=("arbitrary",),
                                             vmem_limit_bytes=VMEM_LIMIT),
        name=f"attn_d{dilation}",
    )(q, k, k, v, v, bias)
    return o.reshape(s, D_MODEL), lse.reshape(s, LANES)


def _alibi_bias(dilation):
    heads = jnp.arange(1, N_HEADS + 1, dtype=F32)
    slopes = jnp.exp2(-ALIBI_MAX_EXP * heads / N_HEADS)
    qi = jnp.arange(BLOCK)[:, None]
    kj = jnp.arange(2 * BLOCK)[None, :]
    dist = qi + BLOCK - kj
    band = (dist >= 0) & (dist <= BLOCK)
    bias = -slopes[:, None, None] * (dilation * dist).astype(F32)[None]
    normal = jnp.where(band[None], bias, NEG_BIG)
    first = jnp.where((band & (kj >= BLOCK))[None], bias, NEG_BIG)
    return jnp.stack([normal, first], axis=0)


def _merge_out_kernel(o0_ref, o1_ref, o2_ref, l0_ref, l1_ref, l2_ref, x1_ref, x1b_ref,
                      wz_ref, expand_ref, w_out_ref, b_out_ref, pg_ref, pb_ref, out_ref):
    lses = [l0_ref[...], l1_ref[...], l2_ref[...]]
    mx = jnp.maximum(jnp.maximum(lses[0], lses[1]), lses[2])
    es = [jnp.exp(l - mx) for l in lses]
    tot = es[0] + es[1] + es[2]
    o_refs = [o0_ref, o1_ref, o2_ref]
    o = None
    for g in range(N_GROUPS):
        w = es[g] / tot
        hi = w.astype(BF16)
        lo = (w - hi.astype(F32)).astype(BF16)
        wide = jnp.dot(jnp.concatenate([hi, lo], axis=1), expand_ref[...],
                       preferred_element_type=F32)
        term = wide * o_refs[g][...]
        o = term if o is None else o + term
    z = jnp.dot(x1b_ref[...], wz_ref[...], preferred_element_type=F32)
    gated = (o * _silu(z)).astype(BF16)
    y = jnp.dot(gated, w_out_ref[...], preferred_element_type=F32) + b_out_ref[...]
    out_ref[...] = _layer_norm(ALPHA * x1_ref[...] + y, pg_ref[...], pb_ref[...])


def _merge_out(os_, lses, x1, x1b, wz, expand, w_out, b_out, pg, pb):
    s = x1.shape[0]
    tm = TM_OUT
    row = pl.BlockSpec((tm, D_MODEL), lambda i: (i, 0))
    stat = pl.BlockSpec((tm, LANES), lambda i: (i, 0))
    full = lambda shape: pl.BlockSpec(shape, lambda i: (0,) * len(shape))
    return pl.pallas_call(
        _merge_out_kernel,
        grid=(s // tm,),
        in_specs=[row, row, row, stat, stat, stat, row, row,
                  full((D_MODEL, D_MODEL)), full((2 * LANES, D_MODEL)), full((D_MODEL, D_MODEL)),
                  full((1, D_MODEL)), full((1, D_MODEL)), full((1, D_MODEL))],
        out_specs=row,
        out_shape=jax.ShapeDtypeStruct((s, D_MODEL), F32),
        compiler_params=pltpu.CompilerParams(dimension_semantics=("arbitrary",),
                                             vmem_limit_bytes=VMEM_LIMIT),
        name="merge_out",
    )(*os_, *lses, x1, x1b, wz, expand, w_out, b_out, pg, pb)


def kernel(x, a_w_in, a_b_in, a_w_dw, a_b_dw, a_ln_g, a_ln_b, a_w_out, a_b_out, kv_w, b_w_in,
           b_w_out, b_b_out, post_ln_g, post_ln_b):
    batch, s, d = x.shape
    assert batch == 1 and d == D_MODEL and a_w_in.shape[0] == 1 and b_w_in.shape[0] == 1
    assert s % (DILATED_GROUPS[-1][1] * TQ_ATTN) == 0
    q_width = N_GROUPS * D_MODEL
    row = lambda t: t.reshape(1, -1)

    w_dw = a_w_dw[0].reshape(CONV_WIDTH, N_SLABS, LANES).transpose(1, 0, 2)
    x1, x1b = _conv_layer(x[0], a_w_in[0].astype(BF16), row(a_b_in[0]), w_dw, row(a_b_dw[0]),
                          row(a_ln_g[0]), row(a_ln_b[0]), a_w_out[0].astype(BF16),
                          row(a_b_out[0]), row(post_ln_g[0]), row(post_ln_b[0]))

    os_, lses = [], []
    for g, (window, dilation) in enumerate(DILATED_GROUPS):
        assert window // dilation == BLOCK
        cols = slice(g * D_MODEL, (g + 1) * D_MODEL)
        w_qkv = jnp.concatenate([b_w_in[0][:, :q_width][:, cols], kv_w[:, :q_width][:, cols],
                                 kv_w[:, q_width:][:, cols]], axis=1).astype(BF16)
        q, k, v = _qkv_proj(x1b, w_qkv, dilation)
        o, lse = _attention_group(q, k, v, _alibi_bias(dilation), dilation)
        os_.append(o)
        lses.append(lse)

    head_of_col = jnp.arange(D_MODEL) // HEAD_DIM
    expand = (jnp.arange(LANES)[:, None] == head_of_col[None, :]).astype(BF16)
    expand = jnp.concatenate([expand, expand], axis=0)
    out = _merge_out(os_, lses, x1, x1b, b_w_in[0][:, q_width:].astype(BF16), expand,
                     b_w_out[0].astype(BF16), row(b_b_out[0]), row(post_ln_g[1]),
                     row(post_ln_b[1]))
    return out[None]
```

```python
import functools

import jax
import jax.numpy as jnp
from jax import lax
from jax.experimental import pallas as pl
from jax.experimental.pallas import tpu as pltpu

F32 = jnp.float32
BF16 = jnp.bfloat16

D_MODEL = 1024
CONV_WIDTH = 31
HEAD_DIM = 64
N_HEADS = 16
N_GROUPS = 3
DILATED_GROUPS = ((128, 1), (512, 4), (2048, 16))
BLOCK = 128
ALIBI_MAX_EXP = 8.0
DEPTH = 2
ALPHA = (2.0 * DEPTH) ** 0.25
LN_EPS = 1e-5
NEG_BIG = -1e30

LANES = 128
N_SLABS = D_MODEL // LANES
HALO = 32
CONV_ROWS = 64
VMEM_LIMIT = 56 * 1024 * 1024
MERGE_DIL = 4
WIDE_DIL = 16

TM_CONV = 256
TM_PROJ = 512
TQ_ATTN = 512
TM_OUT = 512


def _layer_norm(x, g, b):
    mu = jnp.mean(x, axis=-1, keepdims=True)
    xc = x - mu
    var = jnp.mean(xc * xc, axis=-1, keepdims=True)
    return xc * lax.rsqrt(var + LN_EPS) * g + b


def _silu(x):
    return x * jax.nn.sigmoid(x)


def _params(vmem=VMEM_LIMIT):
    return pltpu.CompilerParams(dimension_semantics=("arbitrary",), vmem_limit_bytes=vmem)


def _conv_layer_kernel(x_ref, w_in_ref, b_in_ref, w_dw_ref, b_dw_ref, lng_ref, lnb_ref,
                       w_out_ref, b_out_ref, pg_ref, pb_ref,
                       x1b_ref, x1m_ref, x1bm_ref, x1bw_ref, ubuf, cbuf):
    tm = x_ref.shape[0]

    @pl.when(pl.program_id(0) == 0)
    def _():
        ubuf[:, 0:HALO, :] = jnp.zeros((N_SLABS, HALO, LANES), F32)

    x = x_ref[...]
    h = jnp.dot(x.astype(BF16), w_in_ref[...], preferred_element_type=F32) + b_in_ref[...]
    for c in range(N_SLABS):
        a = h[:, c * LANES:(c + 1) * LANES]
        gate = h[:, D_MODEL + c * LANES:D_MODEL + (c + 1) * LANES]
        ubuf[c, HALO:HALO + tm, :] = a * jax.nn.sigmoid(gate)

    def conv_chunk(idx, carry):
        c = idx // (tm // CONV_ROWS)
        r0 = pl.multiple_of((idx % (tm // CONV_ROWS)) * CONV_ROWS, CONV_ROWS)
        acc = jnp.zeros((CONV_ROWS, LANES), F32)
        for j in range(CONV_WIDTH):
            off = HALO - (CONV_WIDTH - 1) + j
            acc = acc + w_dw_ref[c, j:j + 1, :] * ubuf[c, pl.ds(r0 + off, CONV_ROWS), :]
        cbuf[c, pl.ds(r0, CONV_ROWS), :] = acc
        return carry

    lax.fori_loop(0, N_SLABS * (tm // CONV_ROWS), conv_chunk, 0)
    ubuf[:, 0:HALO, :] = ubuf[:, tm:tm + HALO, :]

    u = jnp.concatenate([cbuf[c] for c in range(N_SLABS)], axis=1) + b_dw_ref[...]
    u = _silu(_layer_norm(u, lng_ref[...], lnb_ref[...]))
    z = h[:, 2 * D_MODEL:]
    gated = (u * _silu(z)).astype(BF16)
    y = jnp.dot(gated, w_out_ref[...], preferred_element_type=F32) + b_out_ref[...]
    x1 = _layer_norm(ALPHA * x + y, pg_ref[...], pb_ref[...])
    x1b_ref[...] = x1.astype(BF16)
    for c in range(N_SLABS):
        cbuf[c] = x1[:, c * LANES:(c + 1) * LANES]
    for c in range(N_SLABS):
        cs = slice(c * LANES, (c + 1) * LANES)
        for r in range(MERGE_DIL):
            part = cbuf[c, pl.ds(r, tm // MERGE_DIL, stride=MERGE_DIL), :]
            x1m_ref[r, :, cs] = part
            x1bm_ref[r, :, cs] = part.astype(BF16)
        for r in range(WIDE_DIL):
            part = cbuf[c, pl.ds(r, tm // WIDE_DIL, stride=WIDE_DIL), :]
            x1bw_ref[r, :, cs] = part.astype(BF16)


def _conv_layer(x, w_in, b_in, w_dw, b_dw, ln_g, ln_b, w_out, b_out, pg, pb):
    s = x.shape[0]
    tm = TM_CONV
    full = lambda shape: pl.BlockSpec(shape, lambda i: (0,) * len(shape))
    row = pl.BlockSpec((tm, D_MODEL), lambda i: (i, 0))
    by_residue = lambda d: pl.BlockSpec((d, tm // d, D_MODEL), lambda i: (0, i, 0))
    return pl.pallas_call(
        _conv_layer_kernel,
        grid=(s // tm,),
        in_specs=[row, full((D_MODEL, 3 * D_MODEL)), full((1, 3 * D_MODEL)),
                  full((N_SLABS, CONV_WIDTH, LANES)), full((1, D_MODEL)), full((1, D_MODEL)),
                  full((1, D_MODEL)), full((D_MODEL, D_MODEL)), full((1, D_MODEL)),
                  full((1, D_MODEL)), full((1, D_MODEL))],
        out_specs=[row, by_residue(MERGE_DIL), by_residue(MERGE_DIL), by_residue(WIDE_DIL)],
        out_shape=[jax.ShapeDtypeStruct((s, D_MODEL), BF16),
                   jax.ShapeDtypeStruct((MERGE_DIL, s // MERGE_DIL, D_MODEL), F32),
                   jax.ShapeDtypeStruct((MERGE_DIL, s // MERGE_DIL, D_MODEL), BF16),
                   jax.ShapeDtypeStruct((WIDE_DIL, s // WIDE_DIL, D_MODEL), BF16)],
        scratch_shapes=[pltpu.VMEM((N_SLABS, HALO + tm, LANES), F32),
                        pltpu.VMEM((N_SLABS, tm, LANES), F32)],
        compiler_params=_params(),
        name="conv_layer",
    )(x, w_in, b_in, w_dw, b_dw, ln_g, ln_b, w_out, b_out, pg, pb)


def _qkv_kernel(x_ref, w_ref, q_ref, k_ref, v_ref):
    h = jnp.dot(x_ref[...], w_ref[...], preferred_element_type=F32)
    q_ref[...] = (h[:, :D_MODEL] * (HEAD_DIM ** -0.5)).astype(BF16)
    k_ref[...] = h[:, D_MODEL:2 * D_MODEL].astype(BF16)
    v_ref[...] = h[:, 2 * D_MODEL:].astype(BF16)


def _qkv_proj(xb, w_qkv, name):
    s = xb.shape[0]
    tm = TM_PROJ
    row = pl.BlockSpec((tm, D_MODEL), lambda i: (i, 0))
    shape = jax.ShapeDtypeStruct((s, D_MODEL), BF16)
    return pl.pallas_call(
        _qkv_kernel,
        grid=(s // tm,),
        in_specs=[row, pl.BlockSpec((D_MODEL, 3 * D_MODEL), lambda i: (0, 0))],
        out_specs=[row, row, row],
        out_shape=[shape, shape, shape],
        compiler_params=_params(),
        name=name,
    )(xb, w_qkv)


def _attn_kernel(q_ref, kp_ref, kc_ref, vp_ref, vc_ref, bias_ref, o_ref, m_ref, den_ref,
                 obuf, mbuf, dbuf, sbuf, pbuf, *, dilation, tiles_per_residue):
    n_sub, rows_sub = q_ref.shape[0], q_ref.shape[1]
    blocks_sub = rows_sub // BLOCK
    first_tile = (pl.program_id(0) % tiles_per_residue) == 0
    lane = lax.broadcasted_iota(jnp.int32, (1, LANES), 1)
    low = lane < HEAD_DIM
    nt = (((1,), (1,)), ((), ()))

    def block(g, b, leading):
        r0 = 0 if leading else pl.multiple_of(b * BLOCK, BLOCK)

        def keys(prev_ref, cur_ref, cs):
            if leading:
                return jnp.concatenate([prev_ref[g, :, cs], cur_ref[g, 0:BLOCK, cs]], axis=0)
            return cur_ref[g, pl.ds(pl.multiple_of((b - 1) * BLOCK, BLOCK), 2 * BLOCK), cs]

        if n_sub == 1:
            out_rows = pl.ds(r0, BLOCK)
        else:
            out_rows = pl.ds(g, BLOCK, stride=n_sub)
        variant = jnp.where(first_tile, 1, 0) if leading else 0
        m_tile = jnp.zeros((BLOCK, LANES), F32)
        den_tile = jnp.ones((BLOCK, LANES), F32)
        for hp in range(N_HEADS // 2):
            cs = slice(hp * LANES, (hp + 1) * LANES)
            q2 = q_ref[g, pl.ds(r0, BLOCK), cs]
            k2 = keys(kp_ref, kc_ref, cs)
            zq = jnp.zeros_like(q2)
            q_pair = jnp.concatenate([jnp.where(low, q2, zq), jnp.where(low, zq, q2)], axis=0)
            s_pair = lax.dot_general(q_pair, k2, nt, preferred_element_type=F32)
            for half in range(2):
                head = 2 * hp + half
                sbuf[head] = s_pair[half * BLOCK:(half + 1) * BLOCK] + bias_ref[variant, head]
        for head in range(N_HEADS):
            s = sbuf[head]
            m = jnp.max(s, axis=-1, keepdims=True)
            p = jnp.exp(s - m)
            den = jnp.sum(p, axis=-1, keepdims=True)
            pbuf[head] = p.astype(BF16)
            m_tile = jnp.where(lane == head, m, m_tile)
            den_tile = jnp.where(lane == head, den, den_tile)
        for hp in range(N_HEADS // 2):
            cs = slice(hp * LANES, (hp + 1) * LANES)
            v2 = keys(vp_ref, vc_ref, cs)
            p_pair = jnp.concatenate([pbuf[2 * hp], pbuf[2 * hp + 1]], axis=0)
            o_pair = jnp.dot(p_pair, v2, preferred_element_type=F32)
            obuf[hp, out_rows, :] = jnp.where(low, o_pair[:BLOCK], o_pair[BLOCK:])
        mbuf[out_rows, :] = m_tile
        dbuf[out_rows, :] = den_tile

    def leading_block(g, carry):
        block(g, 0, True)
        return carry

    def inner_block(b, carry):
        block(0, b, False)
        return carry

    if n_sub == 1:
        block(0, 0, True)
        lax.fori_loop(1, blocks_sub, inner_block, 0)
    else:
        assert blocks_sub == 1
        lax.fori_loop(0, n_sub, leading_block, 0)

    if dilation == 1:
        part = rows_sub // MERGE_DIL
        for r in range(MERGE_DIL):
            for hp in range(N_HEADS // 2):
                o_ref[r, :, hp * LANES:(hp + 1) * LANES] = obuf[hp, pl.ds(r, part, stride=MERGE_DIL), :]
            m_ref[r] = mbuf[pl.ds(r, part, stride=MERGE_DIL), :]
            den_ref[r] = dbuf[pl.ds(r, part, stride=MERGE_DIL), :]
    else:
        for hp in range(N_HEADS // 2):
            o_ref[:, hp * LANES:(hp + 1) * LANES] = obuf[hp]
        m_ref[...] = mbuf[...]
        den_ref[...] = dbuf[...]


def _attention_group(q, k, v, bias, dilation):
    s = q.shape[0]
    m_rows = s // MERGE_DIL
    if dilation == WIDE_DIL:
        n_sub, rows_sub = WIDE_DIL // MERGE_DIL, BLOCK
        view = lambda t: t.reshape(n_sub, MERGE_DIL, s // WIDE_DIL, D_MODEL)
        nj = s // WIDE_DIL // rows_sub
        cur = pl.BlockSpec((n_sub, None, rows_sub, D_MODEL), lambda i: (0, i // nj, i % nj, 0))
        prev = pl.BlockSpec((n_sub, None, BLOCK, D_MODEL),
                            lambda i: (0, i // nj, jnp.maximum(i % nj - 1, 0), 0))
        o_spec = lambda w: pl.BlockSpec((None, n_sub * rows_sub, w), lambda i: (i // nj, i % nj, 0))
    else:
        n_sub, rows_sub = 1, TQ_ATTN
        view = lambda t: t.reshape(1, 1, s, D_MODEL)
        nj = s // dilation // rows_sub
        bpt = rows_sub // BLOCK
        cur = pl.BlockSpec((1, None, rows_sub, D_MODEL), lambda i: (0, 0, i, 0))
        prev = pl.BlockSpec((1, None, BLOCK, D_MODEL),
                            lambda i: (0, 0, jnp.maximum(i * bpt - 1, 0), 0))
        if dilation == MERGE_DIL:
            o_spec = lambda w: pl.BlockSpec((None, rows_sub, w), lambda i: (i // nj, i % nj, 0))
        else:
            assert dilation == 1
            o_spec = lambda w: pl.BlockSpec((MERGE_DIL, rows_sub // MERGE_DIL, w),
                                            lambda i: (0, i, 0))
    tile_rows = n_sub * rows_sub
    q, k, v = view(q), view(k), view(v)
    return pl.pallas_call(
        functools.partial(_attn_kernel, dilation=dilation, tiles_per_residue=nj),
        grid=(s // tile_rows,),
        in_specs=[cur, prev, cur, prev, cur,
                  pl.BlockSpec((2, N_HEADS, BLOCK, 2 * BLOCK), lambda i: (0, 0, 0, 0))],
        out_specs=[o_spec(D_MODEL), o_spec(LANES), o_spec(LANES)],
        out_shape=[jax.ShapeDtypeStruct((MERGE_DIL, m_rows, D_MODEL), F32),
                   jax.ShapeDtypeStruct((MERGE_DIL, m_rows, LANES), F32),
                   jax.ShapeDtypeStruct((MERGE_DIL, m_rows, LANES), F32)],
        scratch_shapes=[pltpu.VMEM((N_HEADS // 2, tile_rows, LANES), F32),
                        pltpu.VMEM((tile_rows, LANES), F32),
                        pltpu.VMEM((tile_rows, LANES), F32),
                        pltpu.VMEM((N_HEADS, BLOCK, 2 * BLOCK), F32),
                        pltpu.VMEM((N_HEADS, BLOCK, 2 * BLOCK), BF16)],
        compiler_params=_params(),
        name=f"attn_d{dilation}",
    )(q, k, k, v, v, bias)


def _alibi_bias(dilation):
    heads = jnp.arange(1, N_HEADS + 1, dtype=F32)
    slopes = jnp.exp2(-ALIBI_MAX_EXP * heads / N_HEADS)
    qi = jnp.arange(BLOCK)[:, None]
    kj = jnp.arange(2 * BLOCK)[None, :]
    dist = qi + BLOCK - kj
    band = (dist >= 0) & (dist <= BLOCK)
    bias = -slopes[:, None, None] * (dilation * dist).astype(F32)[None]
    normal = jnp.where(band[None], bias, NEG_BIG)
    first = jnp.where((band & (kj >= BLOCK))[None], bias, NEG_BIG)
    return jnp.stack([normal, first], axis=0)


def _merge_out_kernel(o0_ref, o1_ref, o2_ref, m0_ref, m1_ref, m2_ref, d0_ref, d1_ref, d2_ref,
                      x1_ref, x1b_ref, wz_ref, expand_ref, w_out_ref, b_out_ref, pg_ref, pb_ref,
                      out_ref, sbuf):
    rows = lambda ref: jnp.concatenate([ref[r] for r in range(MERGE_DIL)], axis=0)
    ms = [rows(m0_ref), rows(m1_ref), rows(m2_ref)]
    dens = [rows(d0_ref), rows(d1_ref), rows(d2_ref)]
    mx = jnp.maximum(jnp.maximum(ms[0], ms[1]), ms[2])
    es = [jnp.exp(m - mx) for m in ms]
    tot = es[0] * dens[0] + es[1] * dens[1] + es[2] * dens[2]
    o_refs = [o0_ref, o1_ref, o2_ref]
    o = None
    for g in range(N_GROUPS):
        w = es[g] / tot
        hi = w.astype(BF16)
        lo = (w - hi.astype(F32)).astype(BF16)
        wide = jnp.dot(jnp.concatenate([hi, lo], axis=1), expand_ref[...],
                       preferred_element_type=F32)
        term = wide * rows(o_refs[g])
        o = term if o is None else o + term
    z = jnp.dot(rows(x1b_ref), wz_ref[...], preferred_element_type=F32)
    gated = (o * _silu(z)).astype(BF16)
    y = jnp.dot(gated, w_out_ref[...], preferred_element_type=F32) + b_out_ref[...]
    res = _layer_norm(ALPHA * rows(x1_ref) + y, pg_ref[...], pb_ref[...])
    part = res.shape[0] // MERGE_DIL
    for c in range(N_SLABS):
        for r in range(MERGE_DIL):
            sbuf[c, pl.ds(r, part, stride=MERGE_DIL), :] = res[r * part:(r + 1) * part,
                                                               c * LANES:(c + 1) * LANES]
    for c in range(N_SLABS):
        out_ref[:, c * LANES:(c + 1) * LANES] = sbuf[c]


def _merge_out(os_, ms, dens, x1m, x1bm, wz, expand, w_out, b_out, pg, pb):
    s = x1m.shape[0] * x1m.shape[1]
    tm = TM_OUT
    by_residue = lambda w: pl.BlockSpec((MERGE_DIL, tm // MERGE_DIL, w), lambda i: (0, i, 0))
    full = lambda shape: pl.BlockSpec(shape, lambda i: (0,) * len(shape))
    wide, stat = by_residue(D_MODEL), by_residue(LANES)
    return pl.pallas_call(
        _merge_out_kernel,
        grid=(s // tm,),
        in_specs=[wide, wide, wide, stat, stat, stat, stat, stat, stat, wide, wide,
                  full((D_MODEL, D_MODEL)), full((2 * LANES, D_MODEL)), full((D_MODEL, D_MODEL)),
                  full((1, D_MODEL)), full((1, D_MODEL)), full((1, D_MODEL))],
        out_specs=pl.BlockSpec((tm, D_MODEL), lambda i: (i, 0)),
        out_shape=jax.ShapeDtypeStruct((s, D_MODEL), F32),
        scratch_shapes=[pltpu.VMEM((N_SLABS, tm, LANES), F32)],
        compiler_params=_params(),
        name="merge_out",
    )(*os_, *ms, *dens, x1m, x1bm, wz, expand, w_out, b_out, pg, pb)


def kernel(x, a_w_in, a_b_in, a_w_dw, a_b_dw, a_ln_g, a_ln_b, a_w_out, a_b_out, kv_w, b_w_in,
           b_w_out, b_b_out, post_ln_g, post_ln_b):
    batch, s, d = x.shape
    assert batch == 1 and d == D_MODEL and a_w_in.shape[0] == 1 and b_w_in.shape[0] == 1
    assert s % (WIDE_DIL * TQ_ATTN) == 0
    assert tuple(dil for _, dil in DILATED_GROUPS) == (1, MERGE_DIL, WIDE_DIL)
    q_width = N_GROUPS * D_MODEL
    row = lambda t: t.reshape(1, -1)

    w_dw = a_w_dw[0].reshape(CONV_WIDTH, N_SLABS, LANES).transpose(1, 0, 2)
    x1b, x1m, x1bm, x1bw = _conv_layer(
        x[0], a_w_in[0].astype(BF16), row(a_b_in[0]), w_dw, row(a_b_dw[0]), row(a_ln_g[0]),
        row(a_ln_b[0]), a_w_out[0].astype(BF16), row(a_b_out[0]), row(post_ln_g[0]),
        row(post_ln_b[0]))
    streams = {1: x1b, MERGE_DIL: x1bm.reshape(s, D_MODEL), WIDE_DIL: x1bw.reshape(s, D_MODEL)}

    os_, ms, dens = [], [], []
    for g, (window, dilation) in enumerate(DILATED_GROUPS):
        assert window // dilation == BLOCK
        cols = slice(g * D_MODEL, (g + 1) * D_MODEL)
        w_qkv = jnp.concatenate([b_w_in[0][:, :q_width][:, cols], kv_w[:, :q_width][:, cols],
                                 kv_w[:, q_width:][:, cols]], axis=1).astype(BF16)
        q, k, v = _qkv_proj(streams[dilation], w_qkv, f"qkv_proj_d{dilation}")
        o, m, den = _attention_group(q, k, v, _alibi_bias(dilation), dilation)
        os_.append(o)
        ms.append(m)
        dens.append(den)

    head_of_col = jnp.arange(D_MODEL) // HEAD_DIM
    expand = (jnp.arange(LANES)[:, None] == head_of_col[None, :]).astype(BF16)
    expand = jnp.concatenate([expand, expand], axis=0)
    out = _merge_out(os_, ms, dens, x1m, x1bm, b_w_in[0][:, q_width:].astype(BF16), expand,
                     b_w_out[0].astype(BF16), row(b_b_out[0]), row(post_ln_g[1]),
                     row(post_ln_b[1]))
    return out[None]
```

```python
import functools

import jax
import numpy as np
import jax.numpy as jnp
from jax import lax
from jax.experimental import pallas as pl
from jax.experimental.pallas import tpu as pltpu

F32 = jnp.float32
BF16 = jnp.bfloat16

D_MODEL = 1024
CONV_WIDTH = 31
HEAD_DIM = 64
N_HEADS = 16
N_GROUPS = 3
DILATED_GROUPS = ((128, 1), (512, 4), (2048, 16))
BLOCK = 128
ALIBI_MAX_EXP = 8.0
DEPTH = 2
ALPHA = (2.0 * DEPTH) ** 0.25
LN_EPS = 1e-5
NEG_BIG = -1e30

LANES = 128
N_SLABS = D_MODEL // LANES
HALO = 32
CONV_ROWS = 64
VMEM_LIMIT = 56 * 1024 * 1024
MERGE_DIL = 4
WIDE_DIL = 16

TM_CONV = 256
TM_PROJ = 512
TQ_ATTN = 512
TM_OUT = 512


def _layer_norm(x, g, b):
    mu = jnp.mean(x, axis=-1, keepdims=True)
    xc = x - mu
    var = jnp.mean(xc * xc, axis=-1, keepdims=True)
    return xc * lax.rsqrt(var + LN_EPS) * g + b


def _silu(x):
    return x * jax.nn.sigmoid(x)


def _params(vmem=VMEM_LIMIT):
    return pltpu.CompilerParams(dimension_semantics=("arbitrary",), vmem_limit_bytes=vmem)


def _conv_layer_kernel(x_ref, w_in_ref, b_in_ref, w_dw_ref, b_dw_ref, lng_ref, lnb_ref,
                       w_out_ref, b_out_ref, pg_ref, pb_ref,
                       x1b_ref, x1m_ref, x1bm_ref, x1bw_ref, ustage, ubuf, cbuf):
    tm = x_ref.shape[0]
    hw = HALO // 2

    @pl.when(pl.program_id(0) == 0)
    def _():
        ubuf[:, :, 0:hw, :] = jnp.zeros((2, N_SLABS, hw, LANES), jnp.uint32)
        ustage[:, 0:8, :] = jnp.zeros((N_SLABS, 8, LANES), F32)

    x = x_ref[...]
    h = jnp.dot(x.astype(BF16), w_in_ref[...], preferred_element_type=F32) + b_in_ref[...]
    for c in range(N_SLABS):
        a = h[:, c * LANES:(c + 1) * LANES]
        gate = h[:, D_MODEL + c * LANES:D_MODEL + (c + 1) * LANES]
        ustage[c, 8:8 + tm, :] = a * jax.nn.sigmoid(gate)
        for shift in range(2):
            rows = ustage[c, 8 - shift:8 - shift + tm, :].astype(BF16)
            ubuf[shift, c, hw:hw + tm // 2, :] = pltpu.bitcast(rows, jnp.uint32)

    def conv_chunk(idx, carry):
        c = idx // (tm // CONV_ROWS)
        w0 = pl.multiple_of((idx % (tm // CONV_ROWS)) * (CONV_ROWS // 2), CONV_ROWS // 2)
        acc = [jnp.zeros((16, LANES), F32) for _ in range(CONV_ROWS // 16)]
        for j in range(CONV_WIDTH):
            off = HALO - (CONV_WIDTH - 1) + j
            shift = off % 2
            w16 = w_dw_ref[c, j].astype(F32)
            for q in range(CONV_ROWS // 16):
                words = ubuf[shift, c, pl.ds(w0 + (off + shift) // 2 + 8 * q, 8), :]
                acc[q] = acc[q] + pltpu.bitcast(words, BF16).astype(F32) * w16
        cbuf[c, pl.ds(2 * w0, CONV_ROWS), :] = jnp.concatenate(acc, axis=0)
        return carry

    lax.fori_loop(0, N_SLABS * (tm // CONV_ROWS), conv_chunk, 0)
    ubuf[:, :, 0:hw, :] = ubuf[:, :, tm // 2:tm // 2 + hw, :]
    ustage[:, 0:8, :] = ustage[:, tm:tm + 8, :]

    u = jnp.concatenate([cbuf[c] for c in range(N_SLABS)], axis=1) + b_dw_ref[...]
    u = _silu(_layer_norm(u, lng_ref[...], lnb_ref[...]))
    z = h[:, 2 * D_MODEL:]
    gated = (u * _silu(z)).astype(BF16)
    y = jnp.dot(gated, w_out_ref[...], preferred_element_type=F32) + b_out_ref[...]
    x1 = _layer_norm(ALPHA * x + y, pg_ref[...], pb_ref[...])
    x1b_ref[...] = x1.astype(BF16)
    for c in range(N_SLABS):
        cbuf[c] = x1[:, c * LANES:(c + 1) * LANES]
    for c in range(N_SLABS):
        cs = slice(c * LANES, (c + 1) * LANES)
        for dil, refs in ((MERGE_DIL, (x1m_ref, x1bm_ref)), (WIDE_DIL, (x1bw_ref,))):
            for r in range(dil):
                rows = cbuf[c, pl.ds(r, tm // dil, stride=dil), :]
                for ref in refs:
                    ref[r, :, cs] = rows.astype(ref.dtype)


def _conv_layer(x, w_in, b_in, w_dw, b_dw, ln_g, ln_b, w_out, b_out, pg, pb):
    s = x.shape[0]
    tm = TM_CONV
    full = lambda shape: pl.BlockSpec(shape, lambda i: (0,) * len(shape))
    row = pl.BlockSpec((tm, D_MODEL), lambda i: (i, 0))
    by_residue = lambda d: pl.BlockSpec((d, tm // d, D_MODEL), lambda i: (0, i, 0))
    return pl.pallas_call(
        _conv_layer_kernel,
        grid=(s // tm,),
        in_specs=[row, full((D_MODEL, 3 * D_MODEL)), full((1, 3 * D_MODEL)),
                  full((N_SLABS, CONV_WIDTH, 16, LANES)), full((1, D_MODEL)), full((1, D_MODEL)),
                  full((1, D_MODEL)), full((D_MODEL, D_MODEL)), full((1, D_MODEL)),
                  full((1, D_MODEL)), full((1, D_MODEL))],
        out_specs=[row, by_residue(MERGE_DIL), by_residue(MERGE_DIL), by_residue(WIDE_DIL)],
        out_shape=[jax.ShapeDtypeStruct((s, D_MODEL), BF16),
                   jax.ShapeDtypeStruct((MERGE_DIL, s // MERGE_DIL, D_MODEL), F32),
                   jax.ShapeDtypeStruct((MERGE_DIL, s // MERGE_DIL, D_MODEL), BF16),
                   jax.ShapeDtypeStruct((WIDE_DIL, s // WIDE_DIL, D_MODEL), BF16)],
        scratch_shapes=[pltpu.VMEM((N_SLABS, 8 + tm, LANES), F32),
                        pltpu.VMEM((2, N_SLABS, (HALO + tm) // 2, LANES), jnp.uint32),
                        pltpu.VMEM((N_SLABS, tm, LANES), F32)],
        compiler_params=_params(),
        name="conv_layer",
    )(x, w_in, b_in, w_dw, b_dw, ln_g, ln_b, w_out, b_out, pg, pb)


def _qkv_kernel(x_ref, w_ref, q_ref, k_ref, v_ref):
    h = jnp.dot(x_ref[...], w_ref[...], preferred_element_type=F32)
    q_ref[...] = (h[:, :D_MODEL] * (HEAD_DIM ** -0.5)).astype(BF16)
    k_ref[...] = h[:, D_MODEL:2 * D_MODEL].astype(BF16)
    v_ref[...] = h[:, 2 * D_MODEL:].astype(BF16)


def _qkv_proj(xb, w_qkv, name):
    s = xb.shape[0]
    tm = TM_PROJ
    row = pl.BlockSpec((tm, D_MODEL), lambda i: (i, 0))
    shape = jax.ShapeDtypeStruct((s, D_MODEL), BF16)
    return pl.pallas_call(
        _qkv_kernel,
        grid=(s // tm,),
        in_specs=[row, pl.BlockSpec((D_MODEL, 3 * D_MODEL), lambda i: (0, 0))],
        out_specs=[row, row, row],
        out_shape=[shape, shape, shape],
        compiler_params=_params(),
        name=name,
    )(xb, w_qkv)


def _attn_kernel(q_ref, kp_ref, kc_ref, vp_ref, vc_ref, bias_ref, o_ref, stat_ref,
                 obuf, tbuf, sbuf, pbuf, *, dilation, tiles_per_residue):
    n_sub, rows_sub = q_ref.shape[0], q_ref.shape[1]
    blocks_sub = rows_sub // BLOCK
    first_tile = (pl.program_id(0) % tiles_per_residue) == 0
    lane = lax.broadcasted_iota(jnp.int32, (1, LANES), 1)
    low = lane < HEAD_DIM
    nt = (((1,), (1,)), ((), ()))

    def block(g, b, leading):
        r0 = 0 if leading else pl.multiple_of(b * BLOCK, BLOCK)

        def keys(prev_ref, cur_ref, cs):
            if leading:
                return jnp.concatenate([prev_ref[g, :, cs], cur_ref[g, 0:BLOCK, cs]], axis=0)
            return cur_ref[g, pl.ds(pl.multiple_of((b - 1) * BLOCK, BLOCK), 2 * BLOCK), cs]

        if n_sub == 1:
            out_rows = pl.ds(r0, BLOCK)
        else:
            out_rows = pl.ds(g, BLOCK, stride=n_sub)
        variant = jnp.where(first_tile, 1, 0) if leading else 0
        stat_tile = jnp.zeros((BLOCK, LANES), F32)
        for hp in range(N_HEADS // 2):
            cs = slice(hp * LANES, (hp + 1) * LANES)
            q2 = q_ref[g, pl.ds(r0, BLOCK), cs]
            k2 = keys(kp_ref, kc_ref, cs)
            zq = jnp.zeros_like(q2)
            q_pair = jnp.concatenate([jnp.where(low, q2, zq), jnp.where(low, zq, q2)], axis=0)
            s_pair = lax.dot_general(q_pair, k2, nt, preferred_element_type=F32)
            for half in range(2):
                head = 2 * hp + half
                sbuf[head] = s_pair[half * BLOCK:(half + 1) * BLOCK] + bias_ref[variant, head]
        for head in range(N_HEADS):
            s = sbuf[head]
            m = jnp.max(s, axis=-1, keepdims=True)
            p = jnp.exp(s - m)
            den = jnp.sum(p, axis=-1, keepdims=True)
            pbuf[head] = p.astype(BF16)
            stat_tile = jnp.where(lane == head, m, stat_tile)
            stat_tile = jnp.where(lane == N_HEADS + head, den, stat_tile)
        for hp in range(N_HEADS // 2):
            cs = slice(hp * LANES, (hp + 1) * LANES)
            v2 = keys(vp_ref, vc_ref, cs)
            p_pair = jnp.concatenate([pbuf[2 * hp], pbuf[2 * hp + 1]], axis=0)
            o_pair = jnp.dot(p_pair, v2, preferred_element_type=F32)
            obuf[hp, out_rows, :] = jnp.where(low, o_pair[:BLOCK], o_pair[BLOCK:])
        tbuf[out_rows, :] = stat_tile

    def leading_block(g, carry):
        block(g, 0, True)
        return carry

    def inner_block(b, carry):
        block(0, b, False)
        return carry

    if n_sub == 1:
        block(0, 0, True)
        lax.fori_loop(1, blocks_sub, inner_block, 0)
    else:
        assert blocks_sub == 1
        lax.fori_loop(0, n_sub, leading_block, 0)

    if dilation == 1:
        part = rows_sub // MERGE_DIL
        for r in range(MERGE_DIL):
            for hp in range(N_HEADS // 2):
                rows = obuf[hp, pl.ds(r, part, stride=MERGE_DIL), :]
                o_ref[r, :, hp * LANES:(hp + 1) * LANES] = rows.astype(BF16)
            stat_ref[r] = tbuf[pl.ds(r, part, stride=MERGE_DIL), :]
    else:
        for hp in range(N_HEADS // 2):
            o_ref[:, hp * LANES:(hp + 1) * LANES] = obuf[hp].astype(BF16)
        stat_ref[...] = tbuf[...]


def _attention_group(q, k, v, bias, dilation):
    s = q.shape[0]
    m_rows = s // MERGE_DIL
    if dilation == WIDE_DIL:
        n_sub, rows_sub = WIDE_DIL // MERGE_DIL, BLOCK
        view = lambda t: t.reshape(n_sub, MERGE_DIL, s // WIDE_DIL, D_MODEL)
        nj = s // WIDE_DIL // rows_sub
        cur = pl.BlockSpec((n_sub, None, rows_sub, D_MODEL), lambda i: (0, i // nj, i % nj, 0))
        prev = pl.BlockSpec((n_sub, None, BLOCK, D_MODEL),
                            lambda i: (0, i // nj, jnp.maximum(i % nj - 1, 0), 0))
        o_spec = lambda w: pl.BlockSpec((None, n_sub * rows_sub, w), lambda i: (i // nj, i % nj, 0))
    else:
        n_sub, rows_sub = 1, TQ_ATTN
        view = lambda t: t.reshape(1, 1, s, D_MODEL)
        nj = s // dilation // rows_sub
        bpt = rows_sub // BLOCK
        cur = pl.BlockSpec((1, None, rows_sub, D_MODEL), lambda i: (0, 0, i, 0))
        prev = pl.BlockSpec((1, None, BLOCK, D_MODEL),
                            lambda i: (0, 0, jnp.maximum(i * bpt - 1, 0), 0))
        if dilation == MERGE_DIL:
            o_spec = lambda w: pl.BlockSpec((None, rows_sub, w), lambda i: (i // nj, i % nj, 0))
        else:
            assert dilation == 1
            o_spec = lambda w: pl.BlockSpec((MERGE_DIL, rows_sub // MERGE_DIL, w),
                                            lambda i: (0, i, 0))
    tile_rows = n_sub * rows_sub
    q, k, v = view(q), view(k), view(v)
    return pl.pallas_call(
        functools.partial(_attn_kernel, dilation=dilation, tiles_per_residue=nj),
        grid=(s // tile_rows,),
        in_specs=[cur, prev, cur, prev, cur,
                  pl.BlockSpec((2, N_HEADS, BLOCK, 2 * BLOCK), lambda i: (0, 0, 0, 0))],
        out_specs=[o_spec(D_MODEL), o_spec(LANES)],
        out_shape=[jax.ShapeDtypeStruct((MERGE_DIL, m_rows, D_MODEL), BF16),
                   jax.ShapeDtypeStruct((MERGE_DIL, m_rows, LANES), F32)],
        scratch_shapes=[pltpu.VMEM((N_HEADS // 2, tile_rows, LANES), F32),
                        pltpu.VMEM((tile_rows, LANES), F32),
                        pltpu.VMEM((N_HEADS, BLOCK, 2 * BLOCK), F32),
                        pltpu.VMEM((N_HEADS, BLOCK, 2 * BLOCK), BF16)],
        compiler_params=_params(),
        name=f"attn_d{dilation}",
    )(q, k, k, v, v, bias)


def _alibi_bias(dilation):
    heads = np.arange(1, N_HEADS + 1, dtype=np.float32)
    slopes = np.exp2(-np.float32(ALIBI_MAX_EXP) * heads / np.float32(N_HEADS))
    qi = np.arange(BLOCK)[:, None]
    kj = np.arange(2 * BLOCK)[None, :]
    dist = qi + BLOCK - kj
    band = (dist >= 0) & (dist <= BLOCK)
    bias = -slopes[:, None, None] * (dilation * dist).astype(np.float32)[None]
    normal = np.where(band[None], bias, np.float32(NEG_BIG))
    first = np.where((band & (kj >= BLOCK))[None], bias, np.float32(NEG_BIG))
    return np.stack([normal, first], axis=0).astype(np.float32)


def _merge_out_kernel(o0_ref, o1_ref, o2_ref, s0_ref, s1_ref, s2_ref,
                      x1_ref, x1b_ref, wz_ref, expand_ref, w_out_ref, b_out_ref, pg_ref, pb_ref,
                      out_ref, sbuf):
    rows = lambda ref: jnp.concatenate([ref[r] for r in range(MERGE_DIL)], axis=0)
    ms = [rows(s0_ref), rows(s1_ref), rows(s2_ref)]
    dens = [pltpu.roll(m, LANES - N_HEADS, axis=1) for m in ms]
    mx = jnp.maximum(jnp.maximum(ms[0], ms[1]), ms[2])
    es = [jnp.exp(m - mx) for m in ms]
    tot = es[0] * dens[0] + es[1] * dens[1] + es[2] * dens[2]
    is_head = lax.broadcasted_iota(jnp.int32, (1, LANES), 1) < N_HEADS
    o_refs = [o0_ref, o1_ref, o2_ref]
    o = None
    for g in range(N_GROUPS):
        w = jnp.where(is_head, es[g] / tot, 0.0)
        hi = w.astype(BF16)
        lo = (w - hi.astype(F32)).astype(BF16)
        wide = jnp.dot(jnp.concatenate([hi, lo], axis=1), expand_ref[...],
                       preferred_element_type=F32)
        term = wide * rows(o_refs[g]).astype(F32)
        o = term if o is None else o + term
    z = jnp.dot(rows(x1b_ref), wz_ref[...], preferred_element_type=F32)
    gated = (o * _silu(z)).astype(BF16)
    y = jnp.dot(gated, w_out_ref[...], preferred_element_type=F32) + b_out_ref[...]
    res = _layer_norm(ALPHA * rows(x1_ref) + y, pg_ref[...], pb_ref[...])
    part = res.shape[0] // MERGE_DIL
    for c in range(N_SLABS):
        for r in range(MERGE_DIL):
            sbuf[c, pl.ds(r, part, stride=MERGE_DIL), :] = res[r * part:(r + 1) * part,
                                                               c * LANES:(c + 1) * LANES]
    for c in range(N_SLABS):
        out_ref[:, c * LANES:(c + 1) * LANES] = sbuf[c]


def _merge_out(os_, stats, x1m, x1bm, wz, expand, w_out, b_out, pg, pb):
    s = x1m.shape[0] * x1m.shape[1]
    tm = TM_OUT
    by_residue = lambda w: pl.BlockSpec((MERGE_DIL, tm // MERGE_DIL, w), lambda i: (0, i, 0))
    full = lambda shape: pl.BlockSpec(shape, lambda i: (0,) * len(shape))
    wide, stat = by_residue(D_MODEL), by_residue(LANES)
    return pl.pallas_call(
        _merge_out_kernel,
        grid=(s // tm,),
        in_specs=[wide, wide, wide, stat, stat, stat, wide, wide,
                  full((D_MODEL, D_MODEL)), full((2 * LANES, D_MODEL)), full((D_MODEL, D_MODEL)),
                  full((1, D_MODEL)), full((1, D_MODEL)), full((1, D_MODEL))],
        out_specs=pl.BlockSpec((tm, D_MODEL), lambda i: (i, 0)),
        out_shape=jax.ShapeDtypeStruct((s, D_MODEL), F32),
        scratch_shapes=[pltpu.VMEM((N_SLABS, tm, LANES), F32)],
        compiler_params=_params(),
        name="merge_out",
    )(*os_, *stats, x1m, x1bm, wz, expand, w_out, b_out, pg, pb)


def kernel(x, a_w_in, a_b_in, a_w_dw, a_b_dw, a_ln_g, a_ln_b, a_w_out, a_b_out, kv_w, b_w_in,
           b_w_out, b_b_out, post_ln_g, post_ln_b):
    batch, s, d = x.shape
    assert batch == 1 and d == D_MODEL and a_w_in.shape[0] == 1 and b_w_in.shape[0] == 1
    assert s % (WIDE_DIL * TQ_ATTN) == 0
    assert tuple(dil for _, dil in DILATED_GROUPS) == (1, MERGE_DIL, WIDE_DIL)
    q_width = N_GROUPS * D_MODEL
    row = lambda t: t.reshape(1, -1)

    w_dw = a_w_dw[0].astype(BF16).reshape(CONV_WIDTH, N_SLABS, 1, LANES).transpose(1, 0, 2, 3)
    w_dw = jnp.broadcast_to(w_dw, (N_SLABS, CONV_WIDTH, 16, LANES))
    x1b, x1m, x1bm, x1bw = _conv_layer(
        x[0], a_w_in[0].astype(BF16), row(a_b_in[0]), w_dw, row(a_b_dw[0]), row(a_ln_g[0]),
        row(a_ln_b[0]), a_w_out[0].astype(BF16), row(a_b_out[0]), row(post_ln_g[0]),
        row(post_ln_b[0]))
    streams = {1: x1b, MERGE_DIL: x1bm.reshape(s, D_MODEL), WIDE_DIL: x1bw.reshape(s, D_MODEL)}

    os_, stats = [], []
    for g, (window, dilation) in enumerate(DILATED_GROUPS):
        assert window // dilation == BLOCK
        cols = slice(g * D_MODEL, (g + 1) * D_MODEL)
        w_qkv = jnp.concatenate([b_w_in[0][:, :q_width][:, cols], kv_w[:, :q_width][:, cols],
                                 kv_w[:, q_width:][:, cols]], axis=1).astype(BF16)
        q, k, v = _qkv_proj(streams[dilation], w_qkv, f"qkv_proj_d{dilation}")
        o, stat = _attention_group(q, k, v, _alibi_bias(dilation), dilation)
        os_.append(o)
        stats.append(stat)

    head_of_col = jnp.arange(D_MODEL) // HEAD_DIM
    expand = (jnp.arange(LANES)[:, None] == head_of_col[None, :]).astype(BF16)
    expand = jnp.concatenate([expand, expand], axis=0)
    out = _merge_out(os_, stats, x1m, x1bm, b_w_in[0][:, q_width:].astype(BF16), expand,
                     b_w_out[0].astype(BF16), row(b_b_out[0]), row(post_ln_g[1]),
                     row(post_ln_b[1]))
    return out[None]
```

```python
import functools

import jax
import numpy as np
import jax.numpy as jnp
from jax import lax
from jax.experimental import pallas as pl
from jax.experimental.pallas import tpu as pltpu

F32 = jnp.float32
BF16 = jnp.bfloat16

D_MODEL = 1024
CONV_WIDTH = 31
HEAD_DIM = 64
N_HEADS = 16
N_GROUPS = 3
DILATED_GROUPS = ((128, 1), (512, 4), (2048, 16))
BLOCK = 128
ALIBI_MAX_EXP = 8.0
DEPTH = 2
ALPHA = (2.0 * DEPTH) ** 0.25
LN_EPS = 1e-5
NEG_BIG = -1e30
LOG2E = 1.4426950408889634

LANES = 128
N_SLABS = D_MODEL // LANES
HALO = 32
CONV_ROWS = 64
VMEM_LIMIT = 56 * 1024 * 1024
MERGE_DIL = 4
WIDE_DIL = 16

TM_CONV = 256
TM_PROJ = 512
TQ_ATTN = 512
TM_OUT = 512


def _layer_norm(x, g, b):
    mu = jnp.mean(x, axis=-1, keepdims=True)
    xc = x - mu
    var = jnp.mean(xc * xc, axis=-1, keepdims=True)
    return xc * lax.rsqrt(var + LN_EPS) * g + b


def _silu(x):
    return x * jax.nn.sigmoid(x)


def _params(vmem=VMEM_LIMIT):
    return pltpu.CompilerParams(dimension_semantics=("arbitrary",), vmem_limit_bytes=vmem)


def _conv_layer_kernel(x_ref, w_in_ref, b_in_ref, w_dw_ref, b_dw_ref, lng_ref, lnb_ref,
                       w_out_ref, b_out_ref, pg_ref, pb_ref,
                       x1m_ref, ustage, ubuf, cbuf):
    tm = x_ref.shape[0]
    hw = HALO // 2

    @pl.when(pl.program_id(0) == 0)
    def _():
        ubuf[:, :, 0:hw, :] = jnp.zeros((2, N_SLABS, hw, LANES), jnp.uint32)
        ustage[:, 0:8, :] = jnp.zeros((N_SLABS, 8, LANES), F32)

    x = x_ref[...]
    h = jnp.dot(x.astype(BF16), w_in_ref[...], preferred_element_type=F32) + b_in_ref[...]
    for c in range(N_SLABS):
        a = h[:, c * LANES:(c + 1) * LANES]
        gate = h[:, D_MODEL + c * LANES:D_MODEL + (c + 1) * LANES]
        ustage[c, 8:8 + tm, :] = a * jax.nn.sigmoid(gate)
        for shift in range(2):
            rows = ustage[c, 8 - shift:8 - shift + tm, :].astype(BF16)
            ubuf[shift, c, hw:hw + tm // 2, :] = pltpu.bitcast(rows, jnp.uint32)

    def conv_chunk(idx, carry):
        c = idx // (tm // CONV_ROWS)
        w0 = pl.multiple_of((idx % (tm // CONV_ROWS)) * (CONV_ROWS // 2), CONV_ROWS // 2)
        acc = [jnp.zeros((16, LANES), F32) for _ in range(CONV_ROWS // 16)]
        for j in range(CONV_WIDTH):
            off = HALO - (CONV_WIDTH - 1) + j
            shift = off % 2
            w16 = w_dw_ref[c, j].astype(F32)
            for q in range(CONV_ROWS // 16):
                words = ubuf[shift, c, pl.ds(w0 + (off + shift) // 2 + 8 * q, 8), :]
                acc[q] = acc[q] + pltpu.bitcast(words, BF16).astype(F32) * w16
        cbuf[c, pl.ds(2 * w0, CONV_ROWS), :] = jnp.concatenate(acc, axis=0)
        return carry

    lax.fori_loop(0, N_SLABS * (tm // CONV_ROWS), conv_chunk, 0)
    ubuf[:, :, 0:hw, :] = ubuf[:, :, tm // 2:tm // 2 + hw, :]
    ustage[:, 0:8, :] = ustage[:, tm:tm + 8, :]

    u = jnp.concatenate([cbuf[c] for c in range(N_SLABS)], axis=1) + b_dw_ref[...]
    u = _silu(_layer_norm(u, lng_ref[...], lnb_ref[...]))
    z = h[:, 2 * D_MODEL:]
    gated = (u * _silu(z)).astype(BF16)
    y = jnp.dot(gated, w_out_ref[...], preferred_element_type=F32) + b_out_ref[...]
    x1 = _layer_norm(ALPHA * x + y, pg_ref[...], pb_ref[...])
    for c in range(N_SLABS):
        cbuf[c] = x1[:, c * LANES:(c + 1) * LANES]
    for c in range(N_SLABS):
        for r in range(MERGE_DIL):
            x1m_ref[r, :, c * LANES:(c + 1) * LANES] = cbuf[
                c, pl.ds(r, tm // MERGE_DIL, stride=MERGE_DIL), :]


def _conv_layer(x, w_in, b_in, w_dw, b_dw, ln_g, ln_b, w_out, b_out, pg, pb):
    s = x.shape[0]
    tm = TM_CONV
    full = lambda shape: pl.BlockSpec(shape, lambda i: (0,) * len(shape))
    row = pl.BlockSpec((tm, D_MODEL), lambda i: (i, 0))
    return pl.pallas_call(
        _conv_layer_kernel,
        grid=(s // tm,),
        in_specs=[row, full((D_MODEL, 3 * D_MODEL)), full((1, 3 * D_MODEL)),
                  full((N_SLABS, CONV_WIDTH, 16, LANES)), full((1, D_MODEL)), full((1, D_MODEL)),
                  full((1, D_MODEL)), full((D_MODEL, D_MODEL)), full((1, D_MODEL)),
                  full((1, D_MODEL)), full((1, D_MODEL))],
        out_specs=pl.BlockSpec((MERGE_DIL, tm // MERGE_DIL, D_MODEL), lambda i: (0, i, 0)),
        out_shape=jax.ShapeDtypeStruct((MERGE_DIL, s // MERGE_DIL, D_MODEL), F32),
        scratch_shapes=[pltpu.VMEM((N_SLABS, 8 + tm, LANES), F32),
                        pltpu.VMEM((2, N_SLABS, (HALO + tm) // 2, LANES), jnp.uint32),
                        pltpu.VMEM((N_SLABS, tm, LANES), F32)],
        compiler_params=_params(),
        name="conv_layer",
    )(x, w_in, b_in, w_dw, b_dw, ln_g, ln_b, w_out, b_out, pg, pb)


def _qkv_kernel(x_ref, w_ref, q_ref, k_ref, v_ref, scr, *, dilation):
    slab = lambda c: slice(c * LANES, (c + 1) * LANES)
    if dilation == MERGE_DIL:
        x = x_ref[...]
    elif dilation == 1:
        part = x_ref.shape[1]
        for c in range(N_SLABS):
            for r in range(MERGE_DIL):
                scr[c, pl.ds(r, part, stride=MERGE_DIL), :] = x_ref[r, :, slab(c)]
        x = jnp.concatenate([scr[c] for c in range(N_SLABS)], axis=1)
    else:
        n_sub = WIDE_DIL // MERGE_DIL
        part = x_ref.shape[0] // n_sub
        for c in range(N_SLABS):
            scr[c] = x_ref[:, slab(c)]
        x = jnp.concatenate(
            [jnp.concatenate([scr[c, pl.ds(sub, part, stride=n_sub), :] for c in range(N_SLABS)],
                             axis=1) for sub in range(n_sub)], axis=0)
    h = jnp.dot(x.astype(BF16), w_ref[...], preferred_element_type=F32)
    outs = (h[:, :D_MODEL] * (HEAD_DIM ** -0.5 * LOG2E), h[:, D_MODEL:2 * D_MODEL],
            h[:, 2 * D_MODEL:])
    for ref, val in zip((q_ref, k_ref, v_ref), outs):
        if dilation == WIDE_DIL:
            for sub in range(n_sub):
                ref[sub] = val[sub * part:(sub + 1) * part].astype(BF16)
        else:
            ref[...] = val.astype(BF16)


def _qkv_proj(x1m, w_qkv, dilation):
    s = x1m.shape[0] * x1m.shape[1]
    tm = TM_PROJ
    out_shape = (s, D_MODEL)
    out = pl.BlockSpec((tm, D_MODEL), lambda i: (i, 0))
    if dilation == MERGE_DIL:
        x, x_spec = x1m.reshape(s, D_MODEL), out
    elif dilation == 1:
        x = x1m
        x_spec = pl.BlockSpec((MERGE_DIL, tm // MERGE_DIL, D_MODEL), lambda i: (0, i, 0))
    else:
        n_sub = WIDE_DIL // MERGE_DIL
        nj = s // MERGE_DIL // tm
        x = x1m
        x_spec = pl.BlockSpec((None, tm, D_MODEL), lambda i: (i // nj, i % nj, 0))
        out_shape = (n_sub, MERGE_DIL, s // WIDE_DIL, D_MODEL)
        out = pl.BlockSpec((n_sub, None, tm // n_sub, D_MODEL),
                           lambda i: (0, i // nj, i % nj, 0))
    shape = jax.ShapeDtypeStruct(out_shape, BF16)
    q, k, v = pl.pallas_call(
        functools.partial(_qkv_kernel, dilation=dilation),
        grid=(s // tm,),
        in_specs=[x_spec, pl.BlockSpec((D_MODEL, 3 * D_MODEL), lambda i: (0, 0))],
        out_specs=[out, out, out],
        out_shape=[shape, shape, shape],
        scratch_shapes=[pltpu.VMEM((N_SLABS, tm, LANES), F32)],
        compiler_params=_params(),
        name=f"qkv_proj_d{dilation}",
    )(x, w_qkv)
    return q.reshape(s, D_MODEL), k.reshape(s, D_MODEL), v.reshape(s, D_MODEL)


def _attn_kernel(q_ref, kp_ref, kc_ref, vp_ref, vc_ref, bias_ref, o_ref, stat_ref,
                 obuf, tbuf, sbuf, pbuf, *, dilation, tiles_per_residue):
    n_sub, rows_sub = q_ref.shape[0], q_ref.shape[1]
    blocks_sub = rows_sub // BLOCK
    first_tile = (pl.program_id(0) % tiles_per_residue) == 0
    lane = lax.broadcasted_iota(jnp.int32, (1, LANES), 1)
    low = lane < HEAD_DIM
    nt = (((1,), (1,)), ((), ()))

    def block(g, b, leading):
        r0 = 0 if leading else pl.multiple_of(b * BLOCK, BLOCK)

        def keys(prev_ref, cur_ref, cs):
            if leading:
                return jnp.concatenate([prev_ref[g, :, cs], cur_ref[g, 0:BLOCK, cs]], axis=0)
            return cur_ref[g, pl.ds(pl.multiple_of((b - 1) * BLOCK, BLOCK), 2 * BLOCK), cs]

        if n_sub == 1:
            out_rows = pl.ds(r0, BLOCK)
        else:
            out_rows = pl.ds(g, BLOCK, stride=n_sub)
        variant = jnp.where(first_tile, 1, 0) if leading else 0
        stat_tile = jnp.zeros((BLOCK, LANES), F32)
        for hp in range(N_HEADS // 2):
            cs = slice(hp * LANES, (hp + 1) * LANES)
            q2 = q_ref[g, pl.ds(r0, BLOCK), cs]
            k2 = keys(kp_ref, kc_ref, cs)
            zq = jnp.zeros_like(q2)
            q_pair = jnp.concatenate([jnp.where(low, q2, zq), jnp.where(low, zq, q2)], axis=0)
            s_pair = lax.dot_general(q_pair, k2, nt, preferred_element_type=F32)
            for half in range(2):
                head = 2 * hp + half
                sbuf[head] = s_pair[half * BLOCK:(half + 1) * BLOCK] + bias_ref[variant, head]
        for head in range(N_HEADS):
            s = sbuf[head]
            m = jnp.max(s, axis=-1, keepdims=True)
            p = jnp.exp2(s - m)
            den = jnp.sum(p, axis=-1, keepdims=True)
            pbuf[head] = p.astype(BF16)
            stat_tile = jnp.where(lane == head, m, stat_tile)
            stat_tile = jnp.where(lane == N_HEADS + head, den, stat_tile)
        for hp in range(N_HEADS // 2):
            cs = slice(hp * LANES, (hp + 1) * LANES)
            v2 = keys(vp_ref, vc_ref, cs)
            p_pair = jnp.concatenate([pbuf[2 * hp], pbuf[2 * hp + 1]], axis=0)
            o_pair = jnp.dot(p_pair, v2, preferred_element_type=F32)
            obuf[hp, out_rows, :] = jnp.where(low, o_pair[:BLOCK], o_pair[BLOCK:])
        tbuf[out_rows, :] = stat_tile

    def leading_block(g, carry):
        block(g, 0, True)
        return carry

    def inner_block(b, carry):
        block(0, b, False)
        return carry

    if n_sub == 1:
        block(0, 0, True)
        lax.fori_loop(1, blocks_sub, inner_block, 0)
    else:
        assert blocks_sub == 1
        lax.fori_loop(0, n_sub, leading_block, 0)

    if dilation == 1:
        part = rows_sub // MERGE_DIL
        for r in range(MERGE_DIL):
            for hp in range(N_HEADS // 2):
                rows = obuf[hp, pl.ds(r, part, stride=MERGE_DIL), :]
                o_ref[r, :, hp * LANES:(hp + 1) * LANES] = rows.astype(BF16)
            stat_ref[r] = tbuf[pl.ds(r, part, stride=MERGE_DIL), :]
    else:
        for hp in range(N_HEADS // 2):
            o_ref[:, hp * LANES:(hp + 1) * LANES] = obuf[hp].astype(BF16)
        stat_ref[...] = tbuf[...]


def _attention_group(q, k, v, bias, dilation):
    s = q.shape[0]
    m_rows = s // MERGE_DIL
    if dilation == WIDE_DIL:
        n_sub, rows_sub = WIDE_DIL // MERGE_DIL, BLOCK
        view = lambda t: t.reshape(n_sub, MERGE_DIL, s // WIDE_DIL, D_MODEL)
        nj = s // WIDE_DIL // rows_sub
        cur = pl.BlockSpec((n_sub, None, rows_sub, D_MODEL), lambda i: (0, i // nj, i % nj, 0))
        prev = pl.BlockSpec((n_sub, None, BLOCK, D_MODEL),
                            lambda i: (0, i // nj, jnp.maximum(i % nj - 1, 0), 0))
        o_spec = lambda w: pl.BlockSpec((None, n_sub * rows_sub, w), lambda i: (i // nj, i % nj, 0))
    else:
        n_sub, rows_sub = 1, TQ_ATTN
        view = lambda t: t.reshape(1, 1, s, D_MODEL)
        nj = s // dilation // rows_sub
        bpt = rows_sub // BLOCK
        cur = pl.BlockSpec((1, None, rows_sub, D_MODEL), lambda i: (0, 0, i, 0))
        prev = pl.BlockSpec((1, None, BLOCK, D_MODEL),
                            lambda i: (0, 0, jnp.maximum(i * bpt - 1, 0), 0))
        if dilation == MERGE_DIL:
            o_spec = lambda w: pl.BlockSpec((None, rows_sub, w), lambda i: (i // nj, i % nj, 0))
        else:
            assert dilation == 1
            o_spec = lambda w: pl.BlockSpec((MERGE_DIL, rows_sub // MERGE_DIL, w),
                                            lambda i: (0, i, 0))
    tile_rows = n_sub * rows_sub
    q, k, v = view(q), view(k), view(v)
    return pl.pallas_call(
        functools.partial(_attn_kernel, dilation=dilation, tiles_per_residue=nj),
        grid=(s // tile_rows,),
        in_specs=[cur, prev, cur, prev, cur,
                  pl.BlockSpec((2, N_HEADS, BLOCK, 2 * BLOCK), lambda i: (0, 0, 0, 0))],
        out_specs=[o_spec(D_MODEL), o_spec(LANES)],
        out_shape=[jax.ShapeDtypeStruct((MERGE_DIL, m_rows, D_MODEL), BF16),
                   jax.ShapeDtypeStruct((MERGE_DIL, m_rows, LANES), F32)],
        scratch_shapes=[pltpu.VMEM((N_HEADS // 2, tile_rows, LANES), F32),
                        pltpu.VMEM((tile_rows, LANES), F32),
                        pltpu.VMEM((N_HEADS, BLOCK, 2 * BLOCK), F32),
                        pltpu.VMEM((N_HEADS, BLOCK, 2 * BLOCK), BF16)],
        compiler_params=_params(),
        name=f"attn_d{dilation}",
    )(q, k, k, v, v, bias)


def _alibi_bias(dilation):
    heads = np.arange(1, N_HEADS + 1, dtype=np.float32)
    slopes = np.exp2(-np.float32(ALIBI_MAX_EXP) * heads / np.float32(N_HEADS))
    qi = np.arange(BLOCK)[:, None]
    kj = np.arange(2 * BLOCK)[None, :]
    dist = qi + BLOCK - kj
    band = (dist >= 0) & (dist <= BLOCK)
    bias = -slopes[:, None, None] * (dilation * dist).astype(np.float32)[None]
    normal = np.where(band[None], bias, np.float32(NEG_BIG))
    first = np.where((band & (kj >= BLOCK))[None], bias, np.float32(NEG_BIG))
    return (np.stack([normal, first], axis=0) * np.float32(LOG2E)).astype(np.float32)


def _merge_out_kernel(o0_ref, o1_ref, o2_ref, s0_ref, s1_ref, s2_ref,
                      x1_ref, wz_ref, expand_ref, w_out_ref, b_out_ref, pg_ref, pb_ref,
                      out_ref, sbuf):
    rows = lambda ref: jnp.concatenate([ref[r] for r in range(MERGE_DIL)], axis=0)
    ms = [rows(s0_ref), rows(s1_ref), rows(s2_ref)]
    dens = [pltpu.roll(m, LANES - N_HEADS, axis=1) for m in ms]
    mx = jnp.maximum(jnp.maximum(ms[0], ms[1]), ms[2])
    es = [jnp.exp2(m - mx) for m in ms]
    tot = es[0] * dens[0] + es[1] * dens[1] + es[2] * dens[2]
    is_head = lax.broadcasted_iota(jnp.int32, (1, LANES), 1) < N_HEADS
    o_refs = [o0_ref, o1_ref, o2_ref]
    o = None
    for g in range(N_GROUPS):
        w = jnp.where(is_head, es[g] / tot, 0.0)
        hi = w.astype(BF16)
        lo = (w - hi.astype(F32)).astype(BF16)
        wide = jnp.dot(jnp.concatenate([hi, lo], axis=1), expand_ref[...],
                       preferred_element_type=F32)
        term = wide * rows(o_refs[g]).astype(F32)
        o = term if o is None else o + term
    x1 = rows(x1_ref)
    z = jnp.dot(x1.astype(BF16), wz_ref[...], preferred_element_type=F32)
    gated = (o * _silu(z)).astype(BF16)
    y = jnp.dot(gated, w_out_ref[...], preferred_element_type=F32) + b_out_ref[...]
    res = _layer_norm(ALPHA * x1 + y, pg_ref[...], pb_ref[...])
    part = res.shape[0] // MERGE_DIL
    for c in range(N_SLABS):
        for r in range(MERGE_DIL):
            sbuf[c, pl.ds(r, part, stride=MERGE_DIL), :] = res[r * part:(r + 1) * part,
                                                               c * LANES:(c + 1) * LANES]
    for c in range(N_SLABS):
        out_ref[:, c * LANES:(c + 1) * LANES] = sbuf[c]


def _merge_out(os_, stats, x1m, wz, expand, w_out, b_out, pg, pb):
    s = x1m.shape[0] * x1m.shape[1]
    tm = TM_OUT
    by_residue = lambda w: pl.BlockSpec((MERGE_DIL, tm // MERGE_DIL, w), lambda i: (0, i, 0))
    full = lambda shape: pl.BlockSpec(shape, lambda i: (0,) * len(shape))
    wide, stat = by_residue(D_MODEL), by_residue(LANES)
    return pl.pallas_call(
        _merge_out_kernel,
        grid=(s // tm,),
        in_specs=[wide, wide, wide, stat, stat, stat, wide,
                  full((D_MODEL, D_MODEL)), full((2 * LANES, D_MODEL)), full((D_MODEL, D_MODEL)),
                  full((1, D_MODEL)), full((1, D_MODEL)), full((1, D_MODEL))],
        out_specs=pl.BlockSpec((tm, D_MODEL), lambda i: (i, 0)),
        out_shape=jax.ShapeDtypeStruct((s, D_MODEL), F32),
        scratch_shapes=[pltpu.VMEM((N_SLABS, tm, LANES), F32)],
        compiler_params=_params(),
        name="merge_out",
    )(*os_, *stats, x1m, wz, expand, w_out, b_out, pg, pb)


def kernel(x, a_w_in, a_b_in, a_w_dw, a_b_dw, a_ln_g, a_ln_b, a_w_out, a_b_out, kv_w, b_w_in,
           b_w_out, b_b_out, post_ln_g, post_ln_b):
    batch, s, d = x.shape
    assert batch == 1 and d == D_MODEL and a_w_in.shape[0] == 1 and b_w_in.shape[0] == 1
    assert s % (WIDE_DIL * TQ_ATTN) == 0
    assert tuple(dil for _, dil in DILATED_GROUPS) == (1, MERGE_DIL, WIDE_DIL)
    q_width = N_GROUPS * D_MODEL
    row = lambda t: t.reshape(1, -1)

    w_dw = a_w_dw[0].astype(BF16).reshape(CONV_WIDTH, N_SLABS, 1, LANES).transpose(1, 0, 2, 3)
    w_dw = jnp.broadcast_to(w_dw, (N_SLABS, CONV_WIDTH, 16, LANES))
    x1m = _conv_layer(
        x[0], a_w_in[0].astype(BF16), row(a_b_in[0]), w_dw, row(a_b_dw[0]), row(a_ln_g[0]),
        row(a_ln_b[0]), a_w_out[0].astype(BF16), row(a_b_out[0]), row(post_ln_g[0]),
        row(post_ln_b[0]))

    os_, stats = [], []
    for g, (window, dilation) in enumerate(DILATED_GROUPS):
        assert window // dilation == BLOCK
        cols = slice(g * D_MODEL, (g + 1) * D_MODEL)
        w_qkv = jnp.concatenate([b_w_in[0][:, :q_width][:, cols], kv_w[:, :q_width][:, cols],
                                 kv_w[:, q_width:][:, cols]], axis=1).astype(BF16)
        q, k, v = _qkv_proj(x1m, w_qkv, dilation)
        o, stat = _attention_group(q, k, v, _alibi_bias(dilation), dilation)
        os_.append(o)
        stats.append(stat)

    head_of_col = jnp.arange(D_MODEL) // HEAD_DIM
    expand = (jnp.arange(LANES)[:, None] == head_of_col[None, :]).astype(BF16)
    expand = jnp.concatenate([expand, expand], axis=0)
    out = _merge_out(os_, stats, x1m, b_w_in[0][:, q_width:].astype(BF16), expand,
                     b_w_out[0].astype(BF16), row(b_b_out[0]), row(post_ln_g[1]),
                     row(post_ln_b[1]))
    return out[None]
```

```python
import functools

import jax
import numpy as np
import jax.numpy as jnp
from jax import lax
from jax.experimental import pallas as pl
from jax.experimental.pallas import tpu as pltpu

F32 = jnp.float32
BF16 = jnp.bfloat16

D_MODEL = 1024
CONV_WIDTH = 31
HEAD_DIM = 64
N_HEADS = 16
N_GROUPS = 3
DILATED_GROUPS = ((128, 1), (512, 4), (2048, 16))
BLOCK = 128
ALIBI_MAX_EXP = 8.0
DEPTH = 2
ALPHA = (2.0 * DEPTH) ** 0.25
LN_EPS = 1e-5
NEG_BIG = -1e30
LOG2E = 1.4426950408889634

LANES = 128
MXU_TILE = 256
N_SLABS = D_MODEL // LANES
HALO = 32
CONV_ROWS = 64
VMEM_LIMIT = 56 * 1024 * 1024
MERGE_DIL = 4
WIDE_DIL = 16

TM_CONV = 256
TM_PROJ = 512
TQ_ATTN = 512
TM_OUT = 512


def _layer_norm(x, g, b):
    mu = jnp.mean(x, axis=-1, keepdims=True)
    xc = x - mu
    var = jnp.mean(xc * xc, axis=-1, keepdims=True)
    return xc * lax.rsqrt(var + LN_EPS) * g + b


def _silu(x):
    return x * jax.nn.sigmoid(x)


def _cast_weight_once(w_ref, wb_ref, col0=0):
    @pl.when(pl.program_id(0) == 0)
    def _():
        for k0 in range(0, w_ref.shape[0], MXU_TILE):
            wb_ref[k0:k0 + MXU_TILE, col0:col0 + w_ref.shape[1]] = (
                w_ref[k0:k0 + MXU_TILE, :].astype(BF16))


def _params(vmem=VMEM_LIMIT):
    return pltpu.CompilerParams(dimension_semantics=("arbitrary",), vmem_limit_bytes=vmem)


def _conv_layer_kernel(x_ref, w_in_ref, b_in_ref, w_dw_ref, b_dw_ref, lng_ref, lnb_ref,
                       w_out_ref, b_out_ref, pg_ref, pb_ref,
                       x1m_ref, ustage, ubuf, cbuf, w_in_b, w_out_b):
    tm = x_ref.shape[0]
    hw = HALO // 2

    @pl.when(pl.program_id(0) == 0)
    def _():
        ubuf[:, :, 0:hw, :] = jnp.zeros((2, N_SLABS, hw, LANES), jnp.uint32)
        ustage[:, 0:8, :] = jnp.zeros((N_SLABS, 8, LANES), F32)

    _cast_weight_once(w_in_ref, w_in_b)
    _cast_weight_once(w_out_ref, w_out_b)
    x = x_ref[...]
    h = jnp.dot(x.astype(BF16), w_in_b[...], preferred_element_type=F32) + b_in_ref[...]
    for c in range(N_SLABS):
        a = h[:, c * LANES:(c + 1) * LANES]
        gate = h[:, D_MODEL + c * LANES:D_MODEL + (c + 1) * LANES]
        ustage[c, 8:8 + tm, :] = a * jax.nn.sigmoid(gate)
        for shift in range(2):
            rows = ustage[c, 8 - shift:8 - shift + tm, :].astype(BF16)
            ubuf[shift, c, hw:hw + tm // 2, :] = pltpu.bitcast(rows, jnp.uint32)

    def conv_chunk(idx, carry):
        c = idx // (tm // CONV_ROWS)
        w0 = pl.multiple_of((idx % (tm // CONV_ROWS)) * (CONV_ROWS // 2), CONV_ROWS // 2)
        acc = [jnp.zeros((16, LANES), F32) for _ in range(CONV_ROWS // 16)]
        for j in range(CONV_WIDTH):
            off = HALO - (CONV_WIDTH - 1) + j
            shift = off % 2
            w16 = w_dw_ref[c, j].astype(F32)
            for q in range(CONV_ROWS // 16):
                words = ubuf[shift, c, pl.ds(w0 + (off + shift) // 2 + 8 * q, 8), :]
                acc[q] = acc[q] + pltpu.bitcast(words, BF16).astype(F32) * w16
        cbuf[c, pl.ds(2 * w0, CONV_ROWS), :] = jnp.concatenate(acc, axis=0)
        return carry

    lax.fori_loop(0, N_SLABS * (tm // CONV_ROWS), conv_chunk, 0, unroll=2)
    ubuf[:, :, 0:hw, :] = ubuf[:, :, tm // 2:tm // 2 + hw, :]
    ustage[:, 0:8, :] = ustage[:, tm:tm + 8, :]

    u = jnp.concatenate([cbuf[c] for c in range(N_SLABS)], axis=1) + b_dw_ref[...]
    u = _silu(_layer_norm(u, lng_ref[...], lnb_ref[...]))
    z = h[:, 2 * D_MODEL:]
    gated = (u * _silu(z)).astype(BF16)
    y = jnp.dot(gated, w_out_b[...], preferred_element_type=F32) + b_out_ref[...]
    x1 = _layer_norm(ALPHA * x + y, pg_ref[...], pb_ref[...])
    for c in range(N_SLABS):
        cbuf[c] = x1[:, c * LANES:(c + 1) * LANES]
    for c in range(N_SLABS):
        for r in range(MERGE_DIL):
            x1m_ref[r, :, c * LANES:(c + 1) * LANES] = cbuf[
                c, pl.ds(r, tm // MERGE_DIL, stride=MERGE_DIL), :]


def _conv_layer(x, w_in, b_in, w_dw, b_dw, ln_g, ln_b, w_out, b_out, pg, pb):
    s = x.shape[0]
    tm = TM_CONV
    full = lambda shape: pl.BlockSpec(shape, lambda i: (0,) * len(shape))
    layer0 = lambda shape: pl.BlockSpec((None,) + shape, lambda i: (0, 0, 0))
    row = pl.BlockSpec((tm, D_MODEL), lambda i: (i, 0))
    return pl.pallas_call(
        _conv_layer_kernel,
        grid=(s // tm,),
        in_specs=[row, layer0((D_MODEL, 3 * D_MODEL)), full((1, 3 * D_MODEL)),
                  full((N_SLABS, CONV_WIDTH, 16, LANES)), full((1, D_MODEL)), full((1, D_MODEL)),
                  full((1, D_MODEL)), layer0((D_MODEL, D_MODEL)), full((1, D_MODEL)),
                  full((1, D_MODEL)), full((1, D_MODEL))],
        out_specs=pl.BlockSpec((MERGE_DIL, tm // MERGE_DIL, D_MODEL), lambda i: (0, i, 0)),
        out_shape=jax.ShapeDtypeStruct((MERGE_DIL, s // MERGE_DIL, D_MODEL), F32),
        scratch_shapes=[pltpu.VMEM((N_SLABS, 8 + tm, LANES), F32),
                        pltpu.VMEM((2, N_SLABS, (HALO + tm) // 2, LANES), jnp.uint32),
                        pltpu.VMEM((N_SLABS, tm, LANES), F32),
                        pltpu.VMEM((D_MODEL, 3 * D_MODEL), BF16),
                        pltpu.VMEM((D_MODEL, D_MODEL), BF16)],
        compiler_params=_params(),
        name="conv_layer",
    )(x, w_in, b_in, w_dw, b_dw, ln_g, ln_b, w_out, b_out, pg, pb)


def _qkv_kernel(x_ref, wq_ref, wk_ref, wv_ref, q_ref, k_ref, v_ref, scr, w_b, *, dilation):
    for j, w_ref in enumerate((wq_ref, wk_ref, wv_ref)):
        _cast_weight_once(w_ref, w_b, j * D_MODEL)
    slab = lambda c: slice(c * LANES, (c + 1) * LANES)
    if dilation == MERGE_DIL:
        x = x_ref[...]
    elif dilation == 1:
        part = x_ref.shape[1]
        for c in range(N_SLABS):
            for r in range(MERGE_DIL):
                scr[c, pl.ds(r, part, stride=MERGE_DIL), :] = x_ref[r, :, slab(c)]
        x = jnp.concatenate([scr[c] for c in range(N_SLABS)], axis=1)
    else:
        n_sub = WIDE_DIL // MERGE_DIL
        part = x_ref.shape[0] // n_sub
        for c in range(N_SLABS):
            scr[c] = x_ref[:, slab(c)]
        x = jnp.concatenate(
            [jnp.concatenate([scr[c, pl.ds(sub, part, stride=n_sub), :] for c in range(N_SLABS)],
                             axis=1) for sub in range(n_sub)], axis=0)
    h = jnp.dot(x.astype(BF16), w_b[...], preferred_element_type=F32)
    outs = (h[:, :D_MODEL] * (HEAD_DIM ** -0.5 * LOG2E), h[:, D_MODEL:2 * D_MODEL],
            h[:, 2 * D_MODEL:])
    for ref, val in zip((q_ref, k_ref, v_ref), outs):
        if dilation == WIDE_DIL:
            for sub in range(n_sub):
                ref[sub] = val[sub * part:(sub + 1) * part].astype(BF16)
        else:
            ref[...] = val.astype(BF16)


def _qkv_proj(x1m, b_w_in, kv_w, group, dilation):
    s = x1m.shape[0] * x1m.shape[1]
    tm = TM_PROJ
    out_shape = (s, D_MODEL)
    out = pl.BlockSpec((tm, D_MODEL), lambda i: (i, 0))
    if dilation == MERGE_DIL:
        x, x_spec = x1m.reshape(s, D_MODEL), out
    elif dilation == 1:
        x = x1m
        x_spec = pl.BlockSpec((MERGE_DIL, tm // MERGE_DIL, D_MODEL), lambda i: (0, i, 0))
    else:
        n_sub = WIDE_DIL // MERGE_DIL
        nj = s // MERGE_DIL // tm
        x = x1m
        x_spec = pl.BlockSpec((None, tm, D_MODEL), lambda i: (i // nj, i % nj, 0))
        out_shape = (n_sub, MERGE_DIL, s // WIDE_DIL, D_MODEL)
        out = pl.BlockSpec((n_sub, None, tm // n_sub, D_MODEL),
                           lambda i: (0, i // nj, i % nj, 0))
    shape = jax.ShapeDtypeStruct(out_shape, BF16)
    q, k, v = pl.pallas_call(
        functools.partial(_qkv_kernel, dilation=dilation),
        grid=(s // tm,),
        in_specs=[x_spec,
                  pl.BlockSpec((None, D_MODEL, D_MODEL), lambda i: (0, 0, group)),
                  pl.BlockSpec((D_MODEL, D_MODEL), lambda i: (0, group)),
                  pl.BlockSpec((D_MODEL, D_MODEL), lambda i: (0, N_GROUPS + group))],
        out_specs=[out, out, out],
        out_shape=[shape, shape, shape],
        scratch_shapes=[pltpu.VMEM((N_SLABS, tm, LANES), F32),
                        pltpu.VMEM((D_MODEL, 3 * D_MODEL), BF16)],
        compiler_params=_params(),
        name=f"qkv_proj_d{dilation}",
    )(x, b_w_in, kv_w, kv_w)
    return q.reshape(s, D_MODEL), k.reshape(s, D_MODEL), v.reshape(s, D_MODEL)


def _attn_kernel(q_ref, kp_ref, kc_ref, vp_ref, vc_ref, bias_ref, o_ref, stat_ref,
                 obuf, tbuf, sbuf, pbuf, *, dilation, tiles_per_residue):
    n_sub, rows_sub = q_ref.shape[0], q_ref.shape[1]
    blocks_sub = rows_sub // BLOCK
    first_tile = (pl.program_id(0) % tiles_per_residue) == 0
    lane = lax.broadcasted_iota(jnp.int32, (1, LANES), 1)
    low = lane < HEAD_DIM
    nt = (((1,), (1,)), ((), ()))

    def block(g, b, leading):
        r0 = 0 if leading else pl.multiple_of(b * BLOCK, BLOCK)

        def keys(prev_ref, cur_ref, cs):
            if leading:
                return jnp.concatenate([prev_ref[g, :, cs], cur_ref[g, 0:BLOCK, cs]], axis=0)
            return cur_ref[g, pl.ds(pl.multiple_of((b - 1) * BLOCK, BLOCK), 2 * BLOCK), cs]

        if n_sub == 1:
            out_rows = pl.ds(r0, BLOCK)
        else:
            out_rows = pl.ds(g, BLOCK, stride=n_sub)
        variant = jnp.where(first_tile, 1, 0) if leading else 0
        stat_tile = jnp.zeros((BLOCK, LANES), F32)
        for hp in range(N_HEADS // 2):
            cs = slice(hp * LANES, (hp + 1) * LANES)
            q2 = q_ref[g, pl.ds(r0, BLOCK), cs]
            k2 = keys(kp_ref, kc_ref, cs)
            zq = jnp.zeros_like(q2)
            q_pair = jnp.concatenate([jnp.where(low, q2, zq), jnp.where(low, zq, q2)], axis=0)
            s_pair = lax.dot_general(q_pair, k2, nt, preferred_element_type=F32)
            for half in range(2):
                head = 2 * hp + half
                sbuf[head] = s_pair[half * BLOCK:(half + 1) * BLOCK] + bias_ref[variant, head]
        for head in range(N_HEADS):
            s = sbuf[head]
            m = jnp.max(s, axis=-1, keepdims=True)
            p = jnp.exp2(s - m)
            den = jnp.sum(p, axis=-1, keepdims=True)
            pbuf[head] = p.astype(BF16)
            stat_tile = jnp.where(lane == head, m, stat_tile)
            stat_tile = jnp.where(lane == N_HEADS + head, den, stat_tile)
        for hp in range(N_HEADS // 2):
            cs = slice(hp * LANES, (hp + 1) * LANES)
            v2 = keys(vp_ref, vc_ref, cs)
            p_pair = jnp.concatenate([pbuf[2 * hp], pbuf[2 * hp + 1]], axis=0)
            o_pair = jnp.dot(p_pair, v2, preferred_element_type=F32)
            obuf[hp, out_rows, :] = jnp.where(low, o_pair[:BLOCK], o_pair[BLOCK:])
        tbuf[out_rows, :] = stat_tile

    def leading_block(g, carry):
        block(g, 0, True)
        return carry

    def inner_block(b, carry):
        block(0, b, False)
        return carry

    if n_sub == 1:
        block(0, 0, True)
        lax.fori_loop(1, blocks_sub, inner_block, 0)
    else:
        assert blocks_sub == 1
        lax.fori_loop(0, n_sub, leading_block, 0)

    if dilation == 1:
        part = rows_sub // MERGE_DIL
        for r in range(MERGE_DIL):
            for hp in range(N_HEADS // 2):
                rows = obuf[hp, pl.ds(r, part, stride=MERGE_DIL), :]
                o_ref[r, :, hp * LANES:(hp + 1) * LANES] = rows.astype(BF16)
            stat_ref[r] = tbuf[pl.ds(r, part, stride=MERGE_DIL), :]
    else:
        for hp in range(N_HEADS // 2):
            o_ref[:, hp * LANES:(hp + 1) * LANES] = obuf[hp].astype(BF16)
        stat_ref[...] = tbuf[...]


def _attention_group(q, k, v, bias, dilation):
    s = q.shape[0]
    m_rows = s // MERGE_DIL
    if dilation == WIDE_DIL:
        n_sub, rows_sub = WIDE_DIL // MERGE_DIL, BLOCK
        view = lambda t: t.reshape(n_sub, MERGE_DIL, s // WIDE_DIL, D_MODEL)
        nj = s // WIDE_DIL // rows_sub
        cur = pl.BlockSpec((n_sub, None, rows_sub, D_MODEL), lambda i: (0, i // nj, i % nj, 0))
        prev = pl.BlockSpec((n_sub, None, BLOCK, D_MODEL),
                            lambda i: (0, i // nj, jnp.maximum(i % nj - 1, 0), 0))
        o_spec = lambda w: pl.BlockSpec((None, n_sub * rows_sub, w), lambda i: (i // nj, i % nj, 0))
    else:
        n_sub, rows_sub = 1, TQ_ATTN
        view = lambda t: t.reshape(1, 1, s, D_MODEL)
        nj = s // dilation // rows_sub
        bpt = rows_sub // BLOCK
        cur = pl.BlockSpec((1, None, rows_sub, D_MODEL), lambda i: (0, 0, i, 0))
        prev = pl.BlockSpec((1, None, BLOCK, D_MODEL),
                            lambda i: (0, 0, jnp.maximum(i * bpt - 1, 0), 0))
        if dilation == MERGE_DIL:
            o_spec = lambda w: pl.BlockSpec((None, rows_sub, w), lambda i: (i // nj, i % nj, 0))
        else:
            assert dilation == 1
            o_spec = lambda w: pl.BlockSpec((MERGE_DIL, rows_sub // MERGE_DIL, w),
                                            lambda i: (0, i, 0))
    tile_rows = n_sub * rows_sub
    q, k, v = view(q), view(k), view(v)
    return pl.pallas_call(
        functools.partial(_attn_kernel, dilation=dilation, tiles_per_residue=nj),
        grid=(s // tile_rows,),
        in_specs=[cur, prev, cur, prev, cur,
                  pl.BlockSpec((2, N_HEADS, BLOCK, 2 * BLOCK), lambda i: (0, 0, 0, 0))],
        out_specs=[o_spec(D_MODEL), o_spec(LANES)],
        out_shape=[jax.ShapeDtypeStruct((MERGE_DIL, m_rows, D_MODEL), BF16),
                   jax.ShapeDtypeStruct((MERGE_DIL, m_rows, LANES), F32)],
        scratch_shapes=[pltpu.VMEM((N_HEADS // 2, tile_rows, LANES), F32),
                        pltpu.VMEM((tile_rows, LANES), F32),
                        pltpu.VMEM((N_HEADS, BLOCK, 2 * BLOCK), F32),
                        pltpu.VMEM((N_HEADS, BLOCK, 2 * BLOCK), BF16)],
        compiler_params=_params(),
        name=f"attn_d{dilation}",
    )(q, k, k, v, v, bias)


def _alibi_bias(dilation):
    heads = np.arange(1, N_HEADS + 1, dtype=np.float32)
    slopes = np.exp2(-np.float32(ALIBI_MAX_EXP) * heads / np.float32(N_HEADS))
    qi = np.arange(BLOCK)[:, None]
    kj = np.arange(2 * BLOCK)[None, :]
    dist = qi + BLOCK - kj
    band = (dist >= 0) & (dist <= BLOCK)
    bias = -slopes[:, None, None] * (dilation * dist).astype(np.float32)[None]
    normal = np.where(band[None], bias, np.float32(NEG_BIG))
    first = np.where((band & (kj >= BLOCK))[None], bias, np.float32(NEG_BIG))
    return (np.stack([normal, first], axis=0) * np.float32(LOG2E)).astype(np.float32)


def _merge_out_kernel(o0_ref, o1_ref, o2_ref, s0_ref, s1_ref, s2_ref,
                      x1_ref, wz_ref, expand_ref, w_out_ref, b_out_ref, pg_ref, pb_ref,
                      out_ref, sbuf, wz_b, w_out_b):
    _cast_weight_once(wz_ref, wz_b)
    _cast_weight_once(w_out_ref, w_out_b)
    rows = lambda ref: jnp.concatenate([ref[r] for r in range(MERGE_DIL)], axis=0)
    ms = [rows(s0_ref), rows(s1_ref), rows(s2_ref)]
    dens = [pltpu.roll(m, LANES - N_HEADS, axis=1) for m in ms]
    mx = jnp.maximum(jnp.maximum(ms[0], ms[1]), ms[2])
    es = [jnp.exp2(m - mx) for m in ms]
    tot = es[0] * dens[0] + es[1] * dens[1] + es[2] * dens[2]
    is_head = lax.broadcasted_iota(jnp.int32, (1, LANES), 1) < N_HEADS
    o_refs = [o0_ref, o1_ref, o2_ref]
    o = None
    for g in range(N_GROUPS):
        w = jnp.where(is_head, es[g] / tot, 0.0)
        hi = w.astype(BF16)
        lo = (w - hi.astype(F32)).astype(BF16)
        wide = jnp.dot(jnp.concatenate([hi, lo], axis=1), expand_ref[...],
                       preferred_element_type=F32)
        term = wide * rows(o_refs[g]).astype(F32)
        o = term if o is None else o + term
    x1 = rows(x1_ref)
    z = jnp.dot(x1.astype(BF16), wz_b[...], preferred_element_type=F32)
    gated = (o * _silu(z)).astype(BF16)
    y = jnp.dot(gated, w_out_b[...], preferred_element_type=F32) + b_out_ref[...]
    res = _layer_norm(ALPHA * x1 + y, pg_ref[...], pb_ref[...])
    part = res.shape[0] // MERGE_DIL
    for c in range(N_SLABS):
        for r in range(MERGE_DIL):
            sbuf[c, pl.ds(r, part, stride=MERGE_DIL), :] = res[r * part:(r + 1) * part,
                                                               c * LANES:(c + 1) * LANES]
    for c in range(N_SLABS):
        out_ref[:, c * LANES:(c + 1) * LANES] = sbuf[c]


def _merge_out(os_, stats, x1m, b_w_in, expand, b_w_out, b_out, pg, pb):
    s = x1m.shape[0] * x1m.shape[1]
    tm = TM_OUT
    by_residue = lambda w: pl.BlockSpec((MERGE_DIL, tm // MERGE_DIL, w), lambda i: (0, i, 0))
    full = lambda shape: pl.BlockSpec(shape, lambda i: (0,) * len(shape))
    wide, stat = by_residue(D_MODEL), by_residue(LANES)
    return pl.pallas_call(
        _merge_out_kernel,
        grid=(s // tm,),
        in_specs=[wide, wide, wide, stat, stat, stat, wide,
                  pl.BlockSpec((None, D_MODEL, D_MODEL), lambda i: (0, 0, N_GROUPS)),
                  full((2 * LANES, D_MODEL)),
                  pl.BlockSpec((None, D_MODEL, D_MODEL), lambda i: (0, 0, 0)),
                  full((1, D_MODEL)), full((1, D_MODEL)), full((1, D_MODEL))],
        out_specs=pl.BlockSpec((tm, D_MODEL), lambda i: (i, 0)),
        out_shape=jax.ShapeDtypeStruct((s, D_MODEL), F32),
        scratch_shapes=[pltpu.VMEM((N_SLABS, tm, LANES), F32),
                        pltpu.VMEM((D_MODEL, D_MODEL), BF16),
                        pltpu.VMEM((D_MODEL, D_MODEL), BF16)],
        compiler_params=_params(),
        name="merge_out",
    )(*os_, *stats, x1m, b_w_in, expand, b_w_out, b_out, pg, pb)


def kernel(x, a_w_in, a_b_in, a_w_dw, a_b_dw, a_ln_g, a_ln_b, a_w_out, a_b_out, kv_w, b_w_in,
           b_w_out, b_b_out, post_ln_g, post_ln_b):
    batch, s, d = x.shape
    assert batch == 1 and d == D_MODEL and a_w_in.shape[0] == 1 and b_w_in.shape[0] == 1
    assert s % (WIDE_DIL * TQ_ATTN) == 0
    assert tuple(dil for _, dil in DILATED_GROUPS) == (1, MERGE_DIL, WIDE_DIL)
    assert kv_w.shape == (D_MODEL, 2 * N_GROUPS * D_MODEL)
    assert b_w_in.shape[1:] == (D_MODEL, (N_GROUPS + 1) * D_MODEL)
    row = lambda t: t.reshape(1, -1)

    w_dw = a_w_dw[0].astype(BF16).reshape(CONV_WIDTH, N_SLABS, 1, LANES).transpose(1, 0, 2, 3)
    w_dw = jnp.broadcast_to(w_dw, (N_SLABS, CONV_WIDTH, 16, LANES))
    x1m = _conv_layer(
        x[0], a_w_in, row(a_b_in[0]), w_dw, row(a_b_dw[0]), row(a_ln_g[0]),
        row(a_ln_b[0]), a_w_out, row(a_b_out[0]), row(post_ln_g[0]),
        row(post_ln_b[0]))

    os_, stats = [], []
    for g, (window, dilation) in enumerate(DILATED_GROUPS):
        assert window // dilation == BLOCK
        q, k, v = _qkv_proj(x1m, b_w_in, kv_w, g, dilation)
        o, stat = _attention_group(q, k, v, _alibi_bias(dilation), dilation)
        os_.append(o)
        stats.append(stat)

    head_of_col = jnp.arange(D_MODEL) // HEAD_DIM
    expand = (jnp.arange(LANES)[:, None] == head_of_col[None, :]).astype(BF16)
    expand = jnp.concatenate([expand, expand], axis=0)
    out = _merge_out(os_, stats, x1m, b_w_in, expand, b_w_out, row(b_b_out[0]), row(post_ln_g[1]),
                     row(post_ln_b[1]))
    return out[None]
```

```python
import functools

import jax
import numpy as np
import jax.numpy as jnp
from jax import lax
from jax.experimental import pallas as pl
from jax.experimental.pallas import tpu as pltpu

F32 = jnp.float32
BF16 = jnp.bfloat16

D_MODEL = 1024
CONV_WIDTH = 31
HEAD_DIM = 64
N_HEADS = 16
N_GROUPS = 3
DILATED_GROUPS = ((128, 1), (512, 4), (2048, 16))
BLOCK = 128
ALIBI_MAX_EXP = 8.0
DEPTH = 2
ALPHA = (2.0 * DEPTH) ** 0.25
LN_EPS = 1e-5
NEG_BIG = -1e30
LOG2E = 1.4426950408889634

LANES = 128
MXU_TILE = 256
N_SLABS = D_MODEL // LANES
HALO = 32
CONV_ROWS = 64
VMEM_LIMIT = 56 * 1024 * 1024
MERGE_DIL = 4
WIDE_DIL = 16

TM_CONV = 512
TM_PROJ = 1024
TQ_ATTN = 1024
TM_OUT = 1024


def _layer_norm(x, g, b):
    mu = jnp.mean(x, axis=-1, keepdims=True)
    xc = x - mu
    var = jnp.mean(xc * xc, axis=-1, keepdims=True)
    return xc * lax.rsqrt(var + LN_EPS) * g + b


def _silu(x):
    return x * jax.nn.sigmoid(x)


def _cast_weight_once(w_ref, wb_ref, col0=0):
    @pl.when(pl.program_id(0) == 0)
    def _():
        for k0 in range(0, w_ref.shape[0], MXU_TILE):
            wb_ref[k0:k0 + MXU_TILE, col0:col0 + w_ref.shape[1]] = (
                w_ref[k0:k0 + MXU_TILE, :].astype(BF16))


def _params(vmem=VMEM_LIMIT):
    return pltpu.CompilerParams(dimension_semantics=("arbitrary",), vmem_limit_bytes=vmem)


def _conv_layer_kernel(x_ref, w_in_ref, b_in_ref, w_dw_ref, b_dw_ref, lng_ref, lnb_ref,
                       w_out_ref, b_out_ref, pg_ref, pb_ref,
                       x1m_ref, ustage, ubuf, cbuf, w_in_b, w_out_b):
    tm = x_ref.shape[0]
    hw = HALO // 2

    @pl.when(pl.program_id(0) == 0)
    def _():
        ubuf[:, :, 0:hw, :] = jnp.zeros((2, N_SLABS, hw, LANES), jnp.uint32)
        ustage[:, 0:8, :] = jnp.zeros((N_SLABS, 8, LANES), F32)

    _cast_weight_once(w_in_ref, w_in_b)
    _cast_weight_once(w_out_ref, w_out_b)
    x = x_ref[...]
    h = jnp.dot(x.astype(BF16), w_in_b[...], preferred_element_type=F32) + b_in_ref[...]
    for c in range(N_SLABS):
        a = h[:, c * LANES:(c + 1) * LANES]
        gate = h[:, D_MODEL + c * LANES:D_MODEL + (c + 1) * LANES]
        ustage[c, 8:8 + tm, :] = a * jax.nn.sigmoid(gate)
        for shift in range(2):
            rows = ustage[c, 8 - shift:8 - shift + tm, :].astype(BF16)
            ubuf[shift, c, hw:hw + tm // 2, :] = pltpu.bitcast(rows, jnp.uint32)

    def conv_chunk(idx, carry):
        c = idx // (tm // CONV_ROWS)
        w0 = pl.multiple_of((idx % (tm // CONV_ROWS)) * (CONV_ROWS // 2), CONV_ROWS // 2)
        acc = [jnp.zeros((16, LANES), F32) for _ in range(CONV_ROWS // 16)]
        for j in range(CONV_WIDTH):
            off = HALO - (CONV_WIDTH - 1) + j
            shift = off % 2
            w16 = w_dw_ref[c, j].astype(F32)
            for q in range(CONV_ROWS // 16):
                words = ubuf[shift, c, pl.ds(w0 + (off + shift) // 2 + 8 * q, 8), :]
                acc[q] = acc[q] + pltpu.bitcast(words, BF16).astype(F32) * w16
        cbuf[c, pl.ds(2 * w0, CONV_ROWS), :] = jnp.concatenate(acc, axis=0)
        return carry

    lax.fori_loop(0, N_SLABS * (tm // CONV_ROWS), conv_chunk, 0, unroll=2)
    ubuf[:, :, 0:hw, :] = ubuf[:, :, tm // 2:tm // 2 + hw, :]
    ustage[:, 0:8, :] = ustage[:, tm:tm + 8, :]

    u = jnp.concatenate([cbuf[c] for c in range(N_SLABS)], axis=1) + b_dw_ref[...]
    u = _silu(_layer_norm(u, lng_ref[...], lnb_ref[...]))
    z = h[:, 2 * D_MODEL:]
    gated = (u * _silu(z)).astype(BF16)
    y = jnp.dot(gated, w_out_b[...], preferred_element_type=F32) + b_out_ref[...]
    x1 = _layer_norm(ALPHA * x + y, pg_ref[...], pb_ref[...])
    for c in range(N_SLABS):
        cbuf[c] = x1[:, c * LANES:(c + 1) * LANES]
    for c in range(N_SLABS):
        for r in range(MERGE_DIL):
            x1m_ref[r, :, c * LANES:(c + 1) * LANES] = cbuf[
                c, pl.ds(r, tm // MERGE_DIL, stride=MERGE_DIL), :]


def _conv_layer(x, w_in, b_in, w_dw, b_dw, ln_g, ln_b, w_out, b_out, pg, pb):
    s = x.shape[0]
    tm = TM_CONV
    full = lambda shape: pl.BlockSpec(shape, lambda i: (0,) * len(shape))
    layer0 = lambda shape: pl.BlockSpec((None,) + shape, lambda i: (0, 0, 0))
    row = pl.BlockSpec((tm, D_MODEL), lambda i: (i, 0))
    return pl.pallas_call(
        _conv_layer_kernel,
        grid=(s // tm,),
        in_specs=[row, layer0((D_MODEL, 3 * D_MODEL)), full((1, 3 * D_MODEL)),
                  full((N_SLABS, CONV_WIDTH, 16, LANES)), full((1, D_MODEL)), full((1, D_MODEL)),
                  full((1, D_MODEL)), layer0((D_MODEL, D_MODEL)), full((1, D_MODEL)),
                  full((1, D_MODEL)), full((1, D_MODEL))],
        out_specs=pl.BlockSpec((MERGE_DIL, tm // MERGE_DIL, D_MODEL), lambda i: (0, i, 0)),
        out_shape=jax.ShapeDtypeStruct((MERGE_DIL, s // MERGE_DIL, D_MODEL), F32),
        scratch_shapes=[pltpu.VMEM((N_SLABS, 8 + tm, LANES), F32),
                        pltpu.VMEM((2, N_SLABS, (HALO + tm) // 2, LANES), jnp.uint32),
                        pltpu.VMEM((N_SLABS, tm, LANES), F32),
                        pltpu.VMEM((D_MODEL, 3 * D_MODEL), BF16),
                        pltpu.VMEM((D_MODEL, D_MODEL), BF16)],
        compiler_params=_params(),
        name="conv_layer",
    )(x, w_in, b_in, w_dw, b_dw, ln_g, ln_b, w_out, b_out, pg, pb)


def _qkv_kernel(x_ref, wq_ref, wk_ref, wv_ref, q_ref, k_ref, v_ref, scr, w_b, *, dilation):
    for j, w_ref in enumerate((wq_ref, wk_ref, wv_ref)):
        _cast_weight_once(w_ref, w_b, j * D_MODEL)
    slab = lambda c: slice(c * LANES, (c + 1) * LANES)
    if dilation == MERGE_DIL:
        x = x_ref[...]
    elif dilation == 1:
        part = x_ref.shape[1]
        for c in range(N_SLABS):
            for r in range(MERGE_DIL):
                scr[c, pl.ds(r, part, stride=MERGE_DIL), :] = x_ref[r, :, slab(c)]
        x = jnp.concatenate([scr[c] for c in range(N_SLABS)], axis=1)
    else:
        n_sub = WIDE_DIL // MERGE_DIL
        part = x_ref.shape[0] // n_sub
        for c in range(N_SLABS):
            scr[c] = x_ref[:, slab(c)]
        x = jnp.concatenate(
            [jnp.concatenate([scr[c, pl.ds(sub, part, stride=n_sub), :] for c in range(N_SLABS)],
                             axis=1) for sub in range(n_sub)], axis=0)
    h = jnp.dot(x.astype(BF16), w_b[...], preferred_element_type=F32)
    outs = (h[:, :D_MODEL] * (HEAD_DIM ** -0.5 * LOG2E), h[:, D_MODEL:2 * D_MODEL],
            h[:, 2 * D_MODEL:])
    for ref, val in zip((q_ref, k_ref, v_ref), outs):
        if dilation == WIDE_DIL:
            for sub in range(n_sub):
                ref[sub] = val[sub * part:(sub + 1) * part].astype(BF16)
        else:
            ref[...] = val.astype(BF16)


def _qkv_proj(x1m, b_w_in, kv_w, group, dilation):
    s = x1m.shape[0] * x1m.shape[1]
    tm = TM_PROJ
    out_shape = (s, D_MODEL)
    out = pl.BlockSpec((tm, D_MODEL), lambda i: (i, 0))
    if dilation == MERGE_DIL:
        x, x_spec = x1m.reshape(s, D_MODEL), out
    elif dilation == 1:
        x = x1m
        x_spec = pl.BlockSpec((MERGE_DIL, tm // MERGE_DIL, D_MODEL), lambda i: (0, i, 0))
    else:
        n_sub = WIDE_DIL // MERGE_DIL
        nj = s // MERGE_DIL // tm
        x = x1m
        x_spec = pl.BlockSpec((None, tm, D_MODEL), lambda i: (i // nj, i % nj, 0))
        out_shape = (n_sub, MERGE_DIL, s // WIDE_DIL, D_MODEL)
        out = pl.BlockSpec((n_sub, None, tm // n_sub, D_MODEL),
                           lambda i: (0, i // nj, i % nj, 0))
    shape = jax.ShapeDtypeStruct(out_shape, BF16)
    q, k, v = pl.pallas_call(
        functools.partial(_qkv_kernel, dilation=dilation),
        grid=(s // tm,),
        in_specs=[x_spec,
                  pl.BlockSpec((None, D_MODEL, D_MODEL), lambda i: (0, 0, group)),
                  pl.BlockSpec((D_MODEL, D_MODEL), lambda i: (0, group)),
                  pl.BlockSpec((D_MODEL, D_MODEL), lambda i: (0, N_GROUPS + group))],
        out_specs=[out, out, out],
        out_shape=[shape, shape, shape],
        scratch_shapes=[pltpu.VMEM((N_SLABS, tm, LANES), F32),
                        pltpu.VMEM((D_MODEL, 3 * D_MODEL), BF16)],
        compiler_params=_params(),
        name=f"qkv_proj_d{dilation}",
    )(x, b_w_in, kv_w, kv_w)
    return q.reshape(s, D_MODEL), k.reshape(s, D_MODEL), v.reshape(s, D_MODEL)


def _attn_kernel(q_ref, kp_ref, kc_ref, vp_ref, vc_ref, bias_ref, o_ref, stat_ref,
                 obuf, tbuf, sbuf, pbuf, *, dilation, tiles_per_residue):
    n_sub, rows_sub = q_ref.shape[0], q_ref.shape[1]
    blocks_sub = rows_sub // BLOCK
    first_tile = (pl.program_id(0) % tiles_per_residue) == 0
    lane = lax.broadcasted_iota(jnp.int32, (1, LANES), 1)
    low = lane < HEAD_DIM
    nt = (((1,), (1,)), ((), ()))

    def block(g, b, leading):
        r0 = 0 if leading else pl.multiple_of(b * BLOCK, BLOCK)

        def keys(prev_ref, cur_ref, cs):
            if leading:
                return jnp.concatenate([prev_ref[g, :, cs], cur_ref[g, 0:BLOCK, cs]], axis=0)
            return cur_ref[g, pl.ds(pl.multiple_of((b - 1) * BLOCK, BLOCK), 2 * BLOCK), cs]

        if n_sub == 1:
            out_rows = pl.ds(r0, BLOCK)
        else:
            out_rows = pl.ds(g, BLOCK, stride=n_sub)
        variant = jnp.where(first_tile, 1, 0) if leading else 0
        stat_tile = jnp.zeros((BLOCK, LANES), F32)
        for hp in range(N_HEADS // 2):
            cs = slice(hp * LANES, (hp + 1) * LANES)
            q2 = q_ref[g, pl.ds(r0, BLOCK), cs]
            k2 = keys(kp_ref, kc_ref, cs)
            zq = jnp.zeros_like(q2)
            q_pair = jnp.concatenate([jnp.where(low, q2, zq), jnp.where(low, zq, q2)], axis=0)
            s_pair = lax.dot_general(q_pair, k2, nt, preferred_element_type=F32)
            for half in range(2):
                head = 2 * hp + half
                sbuf[head] = s_pair[half * BLOCK:(half + 1) * BLOCK] + bias_ref[variant, head]
        for head in range(N_HEADS):
            s = sbuf[head]
            m = jnp.max(s, axis=-1, keepdims=True)
            p = jnp.exp2(s - m)
            den = jnp.sum(p, axis=-1, keepdims=True)
            pbuf[head] = p.astype(BF16)
            stat_tile = jnp.where(lane == head, m, stat_tile)
            stat_tile = jnp.where(lane == N_HEADS + head, den, stat_tile)
        for hp in range(N_HEADS // 2):
            cs = slice(hp * LANES, (hp + 1) * LANES)
            v2 = keys(vp_ref, vc_ref, cs)
            p_pair = jnp.concatenate([pbuf[2 * hp], pbuf[2 * hp + 1]], axis=0)
            o_pair = jnp.dot(p_pair, v2, preferred_element_type=F32)
            obuf[hp, out_rows, :] = jnp.where(low, o_pair[:BLOCK], o_pair[BLOCK:])
        tbuf[out_rows, :] = stat_tile

    def leading_block(g, carry):
        block(g, 0, True)
        return carry

    def inner_block(b, carry):
        block(0, b, False)
        return carry

    if n_sub == 1:
        block(0, 0, True)
        lax.fori_loop(1, blocks_sub, inner_block, 0)
    else:
        assert blocks_sub == 1
        lax.fori_loop(0, n_sub, leading_block, 0)

    if dilation == 1:
        part = rows_sub // MERGE_DIL
        for r in range(MERGE_DIL):
            for hp in range(N_HEADS // 2):
                rows = obuf[hp, pl.ds(r, part, stride=MERGE_DIL), :]
                o_ref[r, :, hp * LANES:(hp + 1) * LANES] = rows.astype(BF16)
            stat_ref[r] = tbuf[pl.ds(r, part, stride=MERGE_DIL), :]
    else:
        for hp in range(N_HEADS // 2):
            o_ref[:, hp * LANES:(hp + 1) * LANES] = obuf[hp].astype(BF16)
        stat_ref[...] = tbuf[...]


def _attention_group(q, k, v, bias, dilation):
    s = q.shape[0]
    m_rows = s // MERGE_DIL
    if dilation == WIDE_DIL:
        n_sub, rows_sub = WIDE_DIL // MERGE_DIL, BLOCK
        view = lambda t: t.reshape(n_sub, MERGE_DIL, s // WIDE_DIL, D_MODEL)
        nj = s // WIDE_DIL // rows_sub
        cur = pl.BlockSpec((n_sub, None, rows_sub, D_MODEL), lambda i: (0, i // nj, i % nj, 0))
        prev = pl.BlockSpec((n_sub, None, BLOCK, D_MODEL),
                            lambda i: (0, i // nj, jnp.maximum(i % nj - 1, 0), 0))
        o_spec = lambda w: pl.BlockSpec((None, n_sub * rows_sub, w), lambda i: (i // nj, i % nj, 0))
    else:
        n_sub, rows_sub = 1, TQ_ATTN
        view = lambda t: t.reshape(1, 1, s, D_MODEL)
        nj = s // dilation // rows_sub
        bpt = rows_sub // BLOCK
        cur = pl.BlockSpec((1, None, rows_sub, D_MODEL), lambda i: (0, 0, i, 0))
        prev = pl.BlockSpec((1, None, BLOCK, D_MODEL),
                            lambda i: (0, 0, jnp.maximum(i * bpt - 1, 0), 0))
        if dilation == MERGE_DIL:
            o_spec = lambda w: pl.BlockSpec((None, rows_sub, w), lambda i: (i // nj, i % nj, 0))
        else:
            assert dilation == 1
            o_spec = lambda w: pl.BlockSpec((MERGE_DIL, rows_sub // MERGE_DIL, w),
                                            lambda i: (0, i, 0))
    tile_rows = n_sub * rows_sub
    q, k, v = view(q), view(k), view(v)
    return pl.pallas_call(
        functools.partial(_attn_kernel, dilation=dilation, tiles_per_residue=nj),
        grid=(s // tile_rows,),
        in_specs=[cur, prev, cur, prev, cur,
                  pl.BlockSpec((2, N_HEADS, BLOCK, 2 * BLOCK), lambda i: (0, 0, 0, 0))],
        out_specs=[o_spec(D_MODEL), o_spec(LANES)],
        out_shape=[jax.ShapeDtypeStruct((MERGE_DIL, m_rows, D_MODEL), BF16),
                   jax.ShapeDtypeStruct((MERGE_DIL, m_rows, LANES), F32)],
        scratch_shapes=[pltpu.VMEM((N_HEADS // 2, tile_rows, LANES), F32),
                        pltpu.VMEM((tile_rows, LANES), F32),
                        pltpu.VMEM((N_HEADS, BLOCK, 2 * BLOCK), F32),
                        pltpu.VMEM((N_HEADS, BLOCK, 2 * BLOCK), BF16)],
        compiler_params=_params(),
        name=f"attn_d{dilation}",
    )(q, k, k, v, v, bias)


def _alibi_bias(dilation):
    heads = np.arange(1, N_HEADS + 1, dtype=np.float32)
    slopes = np.exp2(-np.float32(ALIBI_MAX_EXP) * heads / np.float32(N_HEADS))
    qi = np.arange(BLOCK)[:, None]
    kj = np.arange(2 * BLOCK)[None, :]
    dist = qi + BLOCK - kj
    band = (dist >= 0) & (dist <= BLOCK)
    bias = -slopes[:, None, None] * (dilation * dist).astype(np.float32)[None]
    normal = np.where(band[None], bias, np.float32(NEG_BIG))
    first = np.where((band & (kj >= BLOCK))[None], bias, np.float32(NEG_BIG))
    return (np.stack([normal, first], axis=0) * np.float32(LOG2E)).astype(np.float32)


def _merge_out_kernel(o0_ref, o1_ref, o2_ref, s0_ref, s1_ref, s2_ref,
                      x1_ref, wz_ref, expand_ref, w_out_ref, b_out_ref, pg_ref, pb_ref,
                      out_ref, sbuf, wz_b, w_out_b):
    _cast_weight_once(wz_ref, wz_b)
    _cast_weight_once(w_out_ref, w_out_b)
    rows = lambda ref: jnp.concatenate([ref[r] for r in range(MERGE_DIL)], axis=0)
    ms = [rows(s0_ref), rows(s1_ref), rows(s2_ref)]
    dens = [pltpu.roll(m, LANES - N_HEADS, axis=1) for m in ms]
    mx = jnp.maximum(jnp.maximum(ms[0], ms[1]), ms[2])
    es = [jnp.exp2(m - mx) for m in ms]
    tot = es[0] * dens[0] + es[1] * dens[1] + es[2] * dens[2]
    is_head = lax.broadcasted_iota(jnp.int32, (1, LANES), 1) < N_HEADS
    o_refs = [o0_ref, o1_ref, o2_ref]
    o = None
    for g in range(N_GROUPS):
        w = jnp.where(is_head, es[g] / tot, 0.0)
        hi = w.astype(BF16)
        lo = (w - hi.astype(F32)).astype(BF16)
        wide = jnp.dot(jnp.concatenate([hi, lo], axis=1), expand_ref[...],
                       preferred_element_type=F32)
        term = wide * rows(o_refs[g]).astype(F32)
        o = term if o is None else o + term
    x1 = rows(x1_ref)
    z = jnp.dot(x1.astype(BF16), wz_b[...], preferred_element_type=F32)
    gated = (o * _silu(z)).astype(BF16)
    y = jnp.dot(gated, w_out_b[...], preferred_element_type=F32) + b_out_ref[...]
    res = _layer_norm(ALPHA * x1 + y, pg_ref[...], pb_ref[...])
    part = res.shape[0] // MERGE_DIL
    for c in range(N_SLABS):
        for r in range(MERGE_DIL):
            sbuf[c, pl.ds(r, part, stride=MERGE_DIL), :] = res[r * part:(r + 1) * part,
                                                               c * LANES:(c + 1) * LANES]
    for c in range(N_SLABS):
        out_ref[:, c * LANES:(c + 1) * LANES] = sbuf[c]


def _merge_out(os_, stats, x1m, b_w_in, expand, b_w_out, b_out, pg, pb):
    s = x1m.shape[0] * x1m.shape[1]
    tm = TM_OUT
    by_residue = lambda w: pl.BlockSpec((MERGE_DIL, tm // MERGE_DIL, w), lambda i: (0, i, 0))
    full = lambda shape: pl.BlockSpec(shape, lambda i: (0,) * len(shape))
    wide, stat = by_residue(D_MODEL), by_residue(LANES)
    return pl.pallas_call(
        _merge_out_kernel,
        grid=(s // tm,),
        in_specs=[wide, wide, wide, stat, stat, stat, wide,
                  pl.BlockSpec((None, D_MODEL, D_MODEL), lambda i: (0, 0, N_GROUPS)),
                  full((2 * LANES, D_MODEL)),
                  pl.BlockSpec((None, D_MODEL, D_MODEL), lambda i: (0, 0, 0)),
                  full((1, D_MODEL)), full((1, D_MODEL)), full((1, D_MODEL))],
        out_specs=pl.BlockSpec((tm, D_MODEL), lambda i: (i, 0)),
        out_shape=jax.ShapeDtypeStruct((s, D_MODEL), F32),
        scratch_shapes=[pltpu.VMEM((N_SLABS, tm, LANES), F32),
                        pltpu.VMEM((D_MODEL, D_MODEL), BF16),
                        pltpu.VMEM((D_MODEL, D_MODEL), BF16)],
        compiler_params=_params(),
        name="merge_out",
    )(*os_, *stats, x1m, b_w_in, expand, b_w_out, b_out, pg, pb)


def kernel(x, a_w_in, a_b_in, a_w_dw, a_b_dw, a_ln_g, a_ln_b, a_w_out, a_b_out, kv_w, b_w_in,
           b_w_out, b_b_out, post_ln_g, post_ln_b):
    batch, s, d = x.shape
    assert batch == 1 and d == D_MODEL and a_w_in.shape[0] == 1 and b_w_in.shape[0] == 1
    assert s % (WIDE_DIL * TQ_ATTN) == 0
    assert tuple(dil for _, dil in DILATED_GROUPS) == (1, MERGE_DIL, WIDE_DIL)
    assert kv_w.shape == (D_MODEL, 2 * N_GROUPS * D_MODEL)
    assert b_w_in.shape[1:] == (D_MODEL, (N_GROUPS + 1) * D_MODEL)
    row = lambda t: t.reshape(1, -1)

    w_dw = a_w_dw[0].astype(BF16).reshape(CONV_WIDTH, N_SLABS, 1, LANES).transpose(1, 0, 2, 3)
    w_dw = jnp.broadcast_to(w_dw, (N_SLABS, CONV_WIDTH, 16, LANES))
    x1m = _conv_layer(
        x[0], a_w_in, row(a_b_in[0]), w_dw, row(a_b_dw[0]), row(a_ln_g[0]),
        row(a_ln_b[0]), a_w_out, row(a_b_out[0]), row(post_ln_g[0]),
        row(post_ln_b[0]))

    os_, stats = [], []
    for g, (window, dilation) in enumerate(DILATED_GROUPS):
        assert window // dilation == BLOCK
        q, k, v = _qkv_proj(x1m, b_w_in, kv_w, g, dilation)
        o, stat = _attention_group(q, k, v, _alibi_bias(dilation), dilation)
        os_.append(o)
        stats.append(stat)

    head_of_col = jnp.arange(D_MODEL) // HEAD_DIM
    expand = (jnp.arange(LANES)[:, None] == head_of_col[None, :]).astype(BF16)
    expand = jnp.concatenate([expand, expand], axis=0)
    out = _merge_out(os_, stats, x1m, b_w_in, expand, b_w_out, row(b_b_out[0]), row(post_ln_g[1]),
                     row(post_ln_b[1]))
    return out[None]
```

```python
import functools

import jax
import numpy as np
import jax.numpy as jnp
from jax import lax
from jax.experimental import pallas as pl
from jax.experimental.pallas import tpu as pltpu

F32 = jnp.float32
BF16 = jnp.bfloat16

D_MODEL = 1024
CONV_WIDTH = 31
HEAD_DIM = 64
N_HEADS = 16
N_GROUPS = 3
DILATED_GROUPS = ((128, 1), (512, 4), (2048, 16))
BLOCK = 128
ALIBI_MAX_EXP = 8.0
DEPTH = 2
ALPHA = (2.0 * DEPTH) ** 0.25
LN_EPS = 1e-5
NEG_BIG = -1e30
LOG2E = 1.4426950408889634

LANES = 128
MXU_TILE = 256
N_SLABS = D_MODEL // LANES
HALO = 32
CONV_ROWS = 64
VMEM_LIMIT = 56 * 1024 * 1024
MERGE_DIL = 4
WIDE_DIL = 16

TM_CONV = 512
TM_PROJ = 1024
TQ_ATTN = 1024
TM_OUT = 1024
ATTN_UNROLL = 4


def _layer_norm(x, g, b):
    mu = jnp.mean(x, axis=-1, keepdims=True)
    xc = x - mu
    var = jnp.mean(xc * xc, axis=-1, keepdims=True)
    return xc * lax.rsqrt(var + LN_EPS) * g + b


def _silu(x):
    return x * jax.nn.sigmoid(x)


def _cast_weight_once(w_ref, wb_ref, col0=0):
    @pl.when(pl.program_id(0) == 0)
    def _():
        for k0 in range(0, w_ref.shape[0], MXU_TILE):
            wb_ref[k0:k0 + MXU_TILE, col0:col0 + w_ref.shape[1]] = (
                w_ref[k0:k0 + MXU_TILE, :].astype(BF16))


def _params(vmem=VMEM_LIMIT):
    return pltpu.CompilerParams(dimension_semantics=("arbitrary",), vmem_limit_bytes=vmem)


def _conv_layer_kernel(x_ref, w_in_ref, b_in_ref, w_dw_ref, b_dw_ref, lng_ref, lnb_ref,
                       w_out_ref, b_out_ref, pg_ref, pb_ref,
                       x1m_ref, ustage, ubuf, cbuf, w_in_b, w_out_b):
    tm = x_ref.shape[0]
    hw = HALO // 2

    @pl.when(pl.program_id(0) == 0)
    def _():
        ubuf[:, :, 0:hw, :] = jnp.zeros((2, N_SLABS, hw, LANES), jnp.uint32)
        ustage[:, 0:8, :] = jnp.zeros((N_SLABS, 8, LANES), F32)

    _cast_weight_once(w_in_ref, w_in_b)
    _cast_weight_once(w_out_ref, w_out_b)
    x = x_ref[...]
    h = jnp.dot(x.astype(BF16), w_in_b[...], preferred_element_type=F32) + b_in_ref[...]
    for c in range(N_SLABS):
        a = h[:, c * LANES:(c + 1) * LANES]
        gate = h[:, D_MODEL + c * LANES:D_MODEL + (c + 1) * LANES]
        ustage[c, 8:8 + tm, :] = a * jax.nn.sigmoid(gate)
        for shift in range(2):
            rows = ustage[c, 8 - shift:8 - shift + tm, :].astype(BF16)
            ubuf[shift, c, hw:hw + tm // 2, :] = pltpu.bitcast(rows, jnp.uint32)

    def conv_chunk(idx, carry):
        c = idx // (tm // CONV_ROWS)
        w0 = pl.multiple_of((idx % (tm // CONV_ROWS)) * (CONV_ROWS // 2), CONV_ROWS // 2)
        acc = [jnp.zeros((16, LANES), F32) for _ in range(CONV_ROWS // 16)]
        for j in range(CONV_WIDTH):
            off = HALO - (CONV_WIDTH - 1) + j
            shift = off % 2
            w16 = w_dw_ref[c, j].astype(F32)
            for q in range(CONV_ROWS // 16):
                words = ubuf[shift, c, pl.ds(w0 + (off + shift) // 2 + 8 * q, 8), :]
                acc[q] = acc[q] + pltpu.bitcast(words, BF16).astype(F32) * w16
        cbuf[c, pl.ds(2 * w0, CONV_ROWS), :] = jnp.concatenate(acc, axis=0)
        return carry

    lax.fori_loop(0, N_SLABS * (tm // CONV_ROWS), conv_chunk, 0, unroll=2)
    ubuf[:, :, 0:hw, :] = ubuf[:, :, tm // 2:tm // 2 + hw, :]
    ustage[:, 0:8, :] = ustage[:, tm:tm + 8, :]

    u = jnp.concatenate([cbuf[c] for c in range(N_SLABS)], axis=1) + b_dw_ref[...]
    u = _silu(_layer_norm(u, lng_ref[...], lnb_ref[...]))
    z = h[:, 2 * D_MODEL:]
    gated = (u * _silu(z)).astype(BF16)
    y = jnp.dot(gated, w_out_b[...], preferred_element_type=F32) + b_out_ref[...]
    x1 = _layer_norm(ALPHA * x + y, pg_ref[...], pb_ref[...])
    for c in range(N_SLABS):
        cbuf[c] = x1[:, c * LANES:(c + 1) * LANES]
    for c in range(N_SLABS):
        for r in range(MERGE_DIL):
            x1m_ref[r, :, c * LANES:(c + 1) * LANES] = cbuf[
                c, pl.ds(r, tm // MERGE_DIL, stride=MERGE_DIL), :]


def _conv_layer(x, w_in, b_in, w_dw, b_dw, ln_g, ln_b, w_out, b_out, pg, pb):
    s = x.shape[0]
    tm = TM_CONV
    full = lambda shape: pl.BlockSpec(shape, lambda i: (0,) * len(shape))
    layer0 = lambda shape: pl.BlockSpec((None,) + shape, lambda i: (0, 0, 0))
    row = pl.BlockSpec((tm, D_MODEL), lambda i: (i, 0))
    return pl.pallas_call(
        _conv_layer_kernel,
        grid=(s // tm,),
        in_specs=[row, layer0((D_MODEL, 3 * D_MODEL)), full((1, 3 * D_MODEL)),
                  full((N_SLABS, CONV_WIDTH, 16, LANES)), full((1, D_MODEL)), full((1, D_MODEL)),
                  full((1, D_MODEL)), layer0((D_MODEL, D_MODEL)), full((1, D_MODEL)),
                  full((1, D_MODEL)), full((1, D_MODEL))],
        out_specs=pl.BlockSpec((MERGE_DIL, tm // MERGE_DIL, D_MODEL), lambda i: (0, i, 0)),
        out_shape=jax.ShapeDtypeStruct((MERGE_DIL, s // MERGE_DIL, D_MODEL), F32),
        scratch_shapes=[pltpu.VMEM((N_SLABS, 8 + tm, LANES), F32),
                        pltpu.VMEM((2, N_SLABS, (HALO + tm) // 2, LANES), jnp.uint32),
                        pltpu.VMEM((N_SLABS, tm, LANES), F32),
                        pltpu.VMEM((D_MODEL, 3 * D_MODEL), BF16),
                        pltpu.VMEM((D_MODEL, D_MODEL), BF16)],
        compiler_params=_params(),
        name="conv_layer",
    )(x, w_in, b_in, w_dw, b_dw, ln_g, ln_b, w_out, b_out, pg, pb)


def _qkv_kernel(x_ref, wq_ref, wk_ref, wv_ref, q_ref, k_ref, v_ref, scr, w_b, *, dilation):
    for j, w_ref in enumerate((wq_ref, wk_ref, wv_ref)):
        _cast_weight_once(w_ref, w_b, j * D_MODEL)
    slab = lambda c: slice(c * LANES, (c + 1) * LANES)
    if dilation == MERGE_DIL:
        x = x_ref[...]
    elif dilation == 1:
        part = x_ref.shape[1]
        for c in range(N_SLABS):
            for r in range(MERGE_DIL):
                scr[c, pl.ds(r, part, stride=MERGE_DIL), :] = x_ref[r, :, slab(c)]
        x = jnp.concatenate([scr[c] for c in range(N_SLABS)], axis=1)
    else:
        n_sub = WIDE_DIL // MERGE_DIL
        part = x_ref.shape[0] // n_sub
        for c in range(N_SLABS):
            scr[c] = x_ref[:, slab(c)]
        x = jnp.concatenate(
            [jnp.concatenate([scr[c, pl.ds(sub, part, stride=n_sub), :] for c in range(N_SLABS)],
                             axis=1) for sub in range(n_sub)], axis=0)
    h = jnp.dot(x.astype(BF16), w_b[...], preferred_element_type=F32)
    outs = (h[:, :D_MODEL] * (HEAD_DIM ** -0.5 * LOG2E), h[:, D_MODEL:2 * D_MODEL],
            h[:, 2 * D_MODEL:])
    for ref, val in zip((q_ref, k_ref, v_ref), outs):
        if dilation == WIDE_DIL:
            for sub in range(n_sub):
                ref[sub] = val[sub * part:(sub + 1) * part].astype(BF16)
        else:
            ref[...] = val.astype(BF16)


def _qkv_proj(x1m, b_w_in, kv_w, group, dilation):
    s = x1m.shape[0] * x1m.shape[1]
    tm = TM_PROJ
    out_shape = (s, D_MODEL)
    out = pl.BlockSpec((tm, D_MODEL), lambda i: (i, 0))
    if dilation == MERGE_DIL:
        x, x_spec = x1m.reshape(s, D_MODEL), out
    elif dilation == 1:
        x = x1m
        x_spec = pl.BlockSpec((MERGE_DIL, tm // MERGE_DIL, D_MODEL), lambda i: (0, i, 0))
    else:
        n_sub = WIDE_DIL // MERGE_DIL
        nj = s // MERGE_DIL // tm
        x = x1m
        x_spec = pl.BlockSpec((None, tm, D_MODEL), lambda i: (i // nj, i % nj, 0))
        out_shape = (n_sub, MERGE_DIL, s // WIDE_DIL, D_MODEL)
        out = pl.BlockSpec((n_sub, None, tm // n_sub, D_MODEL),
                           lambda i: (0, i // nj, i % nj, 0))
    shape = jax.ShapeDtypeStruct(out_shape, BF16)
    q, k, v = pl.pallas_call(
        functools.partial(_qkv_kernel, dilation=dilation),
        grid=(s // tm,),
        in_specs=[x_spec,
                  pl.BlockSpec((None, D_MODEL, D_MODEL), lambda i: (0, 0, group)),
                  pl.BlockSpec((D_MODEL, D_MODEL), lambda i: (0, group)),
                  pl.BlockSpec((D_MODEL, D_MODEL), lambda i: (0, N_GROUPS + group))],
        out_specs=[out, out, out],
        out_shape=[shape, shape, shape],
        scratch_shapes=[pltpu.VMEM((N_SLABS, tm, LANES), F32),
                        pltpu.VMEM((D_MODEL, 3 * D_MODEL), BF16)],
        compiler_params=_params(),
        name=f"qkv_proj_d{dilation}",
    )(x, b_w_in, kv_w, kv_w)
    return q.reshape(s, D_MODEL), k.reshape(s, D_MODEL), v.reshape(s, D_MODEL)


def _attn_kernel(q_ref, kp_ref, kc_ref, vp_ref, vc_ref, bias_ref, o_ref, stat_ref,
                 obuf, tbuf, sbuf, pbuf, *, dilation, tiles_per_residue):
    n_sub, rows_sub = q_ref.shape[0], q_ref.shape[1]
    blocks_sub = rows_sub // BLOCK
    first_tile = (pl.program_id(0) % tiles_per_residue) == 0
    lane = lax.broadcasted_iota(jnp.int32, (1, LANES), 1)
    low = lane < HEAD_DIM
    nt = (((1,), (1,)), ((), ()))

    def block(g, b, leading, slot):
        r0 = 0 if leading else pl.multiple_of(b * BLOCK, BLOCK)

        def keys(prev_ref, cur_ref, cs):
            if leading:
                return jnp.concatenate([prev_ref[g, :, cs], cur_ref[g, 0:BLOCK, cs]], axis=0)
            return cur_ref[g, pl.ds(pl.multiple_of((b - 1) * BLOCK, BLOCK), 2 * BLOCK), cs]

        if n_sub == 1:
            out_rows = pl.ds(r0, BLOCK)
        else:
            out_rows = pl.ds(g, BLOCK, stride=n_sub)
        variant = jnp.where(first_tile, 1, 0) if leading else 0
        stat_tile = jnp.zeros((BLOCK, LANES), F32)
        for hp in range(N_HEADS // 2):
            cs = slice(hp * LANES, (hp + 1) * LANES)
            q2 = q_ref[g, pl.ds(r0, BLOCK), cs]
            k2 = keys(kp_ref, kc_ref, cs)
            zq = jnp.zeros_like(q2)
            q_pair = jnp.concatenate([jnp.where(low, q2, zq), jnp.where(low, zq, q2)], axis=0)
            s_pair = lax.dot_general(q_pair, k2, nt, preferred_element_type=F32)
            for half in range(2):
                head = 2 * hp + half
                sbuf[slot, head] = s_pair[half * BLOCK:(half + 1) * BLOCK] + bias_ref[variant, head]
        for head in range(N_HEADS):
            s = sbuf[slot, head]
            m = jnp.max(s, axis=-1, keepdims=True)
            p = jnp.exp2(s - m)
            den = jnp.sum(p, axis=-1, keepdims=True)
            pbuf[slot, head] = p.astype(BF16)
            stat_tile = jnp.where(lane == head, m, stat_tile)
            stat_tile = jnp.where(lane == N_HEADS + head, den, stat_tile)
        for hp in range(N_HEADS // 2):
            cs = slice(hp * LANES, (hp + 1) * LANES)
            v2 = keys(vp_ref, vc_ref, cs)
            p_pair = jnp.concatenate([pbuf[slot, 2 * hp], pbuf[slot, 2 * hp + 1]], axis=0)
            o_pair = jnp.dot(p_pair, v2, preferred_element_type=F32)
            obuf[hp, out_rows, :] = jnp.where(low, o_pair[:BLOCK], o_pair[BLOCK:])
        tbuf[out_rows, :] = stat_tile

    def leading_group(j, carry):
        for u in range(ATTN_UNROLL):
            block(ATTN_UNROLL * j + u, 0, True, u)
        return carry

    def inner_group(j, carry):
        for u in range(ATTN_UNROLL):
            block(0, ATTN_UNROLL * j + u, False, u)
        return carry

    if n_sub == 1:
        assert blocks_sub % ATTN_UNROLL == 0
        block(0, 0, True, 0)
        for u in range(1, ATTN_UNROLL):
            block(0, u, False, u)
        lax.fori_loop(1, blocks_sub // ATTN_UNROLL, inner_group, 0)
    else:
        assert blocks_sub == 1 and n_sub % ATTN_UNROLL == 0
        lax.fori_loop(0, n_sub // ATTN_UNROLL, leading_group, 0)

    if dilation == 1:
        part = rows_sub // MERGE_DIL
        for r in range(MERGE_DIL):
            for hp in range(N_HEADS // 2):
                rows = obuf[hp, pl.ds(r, part, stride=MERGE_DIL), :]
                o_ref[r, :, hp * LANES:(hp + 1) * LANES] = rows.astype(BF16)
            stat_ref[r] = tbuf[pl.ds(r, part, stride=MERGE_DIL), :]
    else:
        for hp in range(N_HEADS // 2):
            o_ref[:, hp * LANES:(hp + 1) * LANES] = obuf[hp].astype(BF16)
        stat_ref[...] = tbuf[...]


def _attention_group(q, k, v, bias, dilation):
    s = q.shape[0]
    m_rows = s // MERGE_DIL
    if dilation == WIDE_DIL:
        n_sub, rows_sub = WIDE_DIL // MERGE_DIL, BLOCK
        view = lambda t: t.reshape(n_sub, MERGE_DIL, s // WIDE_DIL, D_MODEL)
        nj = s // WIDE_DIL // rows_sub
        cur = pl.BlockSpec((n_sub, None, rows_sub, D_MODEL), lambda i: (0, i // nj, i % nj, 0))
        prev = pl.BlockSpec((n_sub, None, BLOCK, D_MODEL),
                            lambda i: (0, i // nj, jnp.maximum(i % nj - 1, 0), 0))
        o_spec = lambda w: pl.BlockSpec((None, n_sub * rows_sub, w), lambda i: (i // nj, i % nj, 0))
    else:
        n_sub, rows_sub = 1, TQ_ATTN
        view = lambda t: t.reshape(1, 1, s, D_MODEL)
        nj = s // dilation // rows_sub
        bpt = rows_sub // BLOCK
        cur = pl.BlockSpec((1, None, rows_sub, D_MODEL), lambda i: (0, 0, i, 0))
        prev = pl.BlockSpec((1, None, BLOCK, D_MODEL),
                            lambda i: (0, 0, jnp.maximum(i * bpt - 1, 0), 0))
        if dilation == MERGE_DIL:
            o_spec = lambda w: pl.BlockSpec((None, rows_sub, w), lambda i: (i // nj, i % nj, 0))
        else:
            assert dilation == 1
            o_spec = lambda w: pl.BlockSpec((MERGE_DIL, rows_sub // MERGE_DIL, w),
                                            lambda i: (0, i, 0))
    tile_rows = n_sub * rows_sub
    q, k, v = view(q), view(k), view(v)
    return pl.pallas_call(
        functools.partial(_attn_kernel, dilation=dilation, tiles_per_residue=nj),
        grid=(s // tile_rows,),
        in_specs=[cur, prev, cur, prev, cur,
                  pl.BlockSpec((2, N_HEADS, BLOCK, 2 * BLOCK), lambda i: (0, 0, 0, 0))],
        out_specs=[o_spec(D_MODEL), o_spec(LANES)],
        out_shape=[jax.ShapeDtypeStruct((MERGE_DIL, m_rows, D_MODEL), BF16),
                   jax.ShapeDtypeStruct((MERGE_DIL, m_rows, LANES), F32)],
        scratch_shapes=[pltpu.VMEM((N_HEADS // 2, tile_rows, LANES), F32),
                        pltpu.VMEM((tile_rows, LANES), F32),
                        pltpu.VMEM((ATTN_UNROLL, N_HEADS, BLOCK, 2 * BLOCK), F32),
                        pltpu.VMEM((ATTN_UNROLL, N_HEADS, BLOCK, 2 * BLOCK), BF16)],
        compiler_params=_params(),
        name=f"attn_d{dilation}",
    )(q, k, k, v, v, bias)


def _alibi_bias(dilation):
    heads = np.arange(1, N_HEADS + 1, dtype=np.float32)
    slopes = np.exp2(-np.float32(ALIBI_MAX_EXP) * heads / np.float32(N_HEADS))
    qi = np.arange(BLOCK)[:, None]
    kj = np.arange(2 * BLOCK)[None, :]
    dist = qi + BLOCK - kj
    band = (dist >= 0) & (dist <= BLOCK)
    bias = -slopes[:, None, None] * (dilation * dist).astype(np.float32)[None]
    normal = np.where(band[None], bias, np.float32(NEG_BIG))
    first = np.where((band & (kj >= BLOCK))[None], bias, np.float32(NEG_BIG))
    return (np.stack([normal, first], axis=0) * np.float32(LOG2E)).astype(np.float32)


def _merge_out_kernel(o0_ref, o1_ref, o2_ref, s0_ref, s1_ref, s2_ref,
                      x1_ref, wz_ref, expand_ref, w_out_ref, b_out_ref, pg_ref, pb_ref,
                      out_ref, sbuf, wz_b, w_out_b):
    _cast_weight_once(wz_ref, wz_b)
    _cast_weight_once(w_out_ref, w_out_b)
    rows = lambda ref: jnp.concatenate([ref[r] for r in range(MERGE_DIL)], axis=0)
    ms = [rows(s0_ref), rows(s1_ref), rows(s2_ref)]
    dens = [pltpu.roll(m, LANES - N_HEADS, axis=1) for m in ms]
    mx = jnp.maximum(jnp.maximum(ms[0], ms[1]), ms[2])
    es = [jnp.exp2(m - mx) for m in ms]
    tot = es[0] * dens[0] + es[1] * dens[1] + es[2] * dens[2]
    is_head = lax.broadcasted_iota(jnp.int32, (1, LANES), 1) < N_HEADS
    o_refs = [o0_ref, o1_ref, o2_ref]
    o = None
    for g in range(N_GROUPS):
        w = jnp.where(is_head, es[g] / tot, 0.0)
        hi = w.astype(BF16)
        lo = (w - hi.astype(F32)).astype(BF16)
        wide = jnp.dot(jnp.concatenate([hi, lo], axis=1), expand_ref[...],
                       preferred_element_type=F32)
        term = wide * rows(o_refs[g]).astype(F32)
        o = term if o is None else o + term
    x1 = rows(x1_ref)
    z = jnp.dot(x1.astype(BF16), wz_b[...], preferred_element_type=F32)
    gated = (o * _silu(z)).astype(BF16)
    y = jnp.dot(gated, w_out_b[...], preferred_element_type=F32) + b_out_ref[...]
    res = _layer_norm(ALPHA * x1 + y, pg_ref[...], pb_ref[...])
    part = res.shape[0] // MERGE_DIL
    for c in range(N_SLABS):
        for r in range(MERGE_DIL):
            sbuf[c, pl.ds(r, part, stride=MERGE_DIL), :] = res[r * part:(r + 1) * part,
                                                               c * LANES:(c + 1) * LANES]
    for c in range(N_SLABS):
        out_ref[:, c * LANES:(c + 1) * LANES] = sbuf[c]


def _merge_out(os_, stats, x1m, b_w_in, expand, b_w_out, b_out, pg, pb):
    s = x1m.shape[0] * x1m.shape[1]
    tm = TM_OUT
    by_residue = lambda w: pl.BlockSpec((MERGE_DIL, tm // MERGE_DIL, w), lambda i: (0, i, 0))
    full = lambda shape: pl.BlockSpec(shape, lambda i: (0,) * len(shape))
    wide, stat = by_residue(D_MODEL), by_residue(LANES)
    return pl.pallas_call(
        _merge_out_kernel,
        grid=(s // tm,),
        in_specs=[wide, wide, wide, stat, stat, stat, wide,
                  pl.BlockSpec((None, D_MODEL, D_MODEL), lambda i: (0, 0, N_GROUPS)),
                  full((2 * LANES, D_MODEL)),
                  pl.BlockSpec((None, D_MODEL, D_MODEL), lambda i: (0, 0, 0)),
                  full((1, D_MODEL)), full((1, D_MODEL)), full((1, D_MODEL))],
        out_specs=pl.BlockSpec((tm, D_MODEL), lambda i: (i, 0)),
        out_shape=jax.ShapeDtypeStruct((s, D_MODEL), F32),
        scratch_shapes=[pltpu.VMEM((N_SLABS, tm, LANES), F32),
                        pltpu.VMEM((D_MODEL, D_MODEL), BF16),
                        pltpu.VMEM((D_MODEL, D_MODEL), BF16)],
        compiler_params=_params(),
        name="merge_out",
    )(*os_, *stats, x1m, b_w_in, expand, b_w_out, b_out, pg, pb)


def kernel(x, a_w_in, a_b_in, a_w_dw, a_b_dw, a_ln_g, a_ln_b, a_w_out, a_b_out, kv_w, b_w_in,
           b_w_out, b_b_out, post_ln_g, post_ln_b):
    batch, s, d = x.shape
    assert batch == 1 and d == D_MODEL and a_w_in.shape[0] == 1 and b_w_in.shape[0] == 1
    assert s % (WIDE_DIL * TQ_ATTN) == 0
    assert tuple(dil for _, dil in DILATED_GROUPS) == (1, MERGE_DIL, WIDE_DIL)
    assert kv_w.shape == (D_MODEL, 2 * N_GROUPS * D_MODEL)
    assert b_w_in.shape[1:] == (D_MODEL, (N_GROUPS + 1) * D_MODEL)
    row = lambda t: t.reshape(1, -1)

    w_dw = a_w_dw[0].astype(BF16).reshape(CONV_WIDTH, N_SLABS, 1, LANES).transpose(1, 0, 2, 3)
    w_dw = jnp.broadcast_to(w_dw, (N_SLABS, CONV_WIDTH, 16, LANES))
    x1m = _conv_layer(
        x[0], a_w_in, row(a_b_in[0]), w_dw, row(a_b_dw[0]), row(a_ln_g[0]),
        row(a_ln_b[0]), a_w_out, row(a_b_out[0]), row(post_ln_g[0]),
        row(post_ln_b[0]))

    os_, stats = [], []
    for g, (window, dilation) in enumerate(DILATED_GROUPS):
        assert window // dilation == BLOCK
        q, k, v = _qkv_proj(x1m, b_w_in, kv_w, g, dilation)
        o, stat = _attention_group(q, k, v, _alibi_bias(dilation), dilation)
        os_.append(o)
        stats.append(stat)

    head_of_col = jnp.arange(D_MODEL) // HEAD_DIM
    expand = (jnp.arange(LANES)[:, None] == head_of_col[None, :]).astype(BF16)
    expand = jnp.concatenate([expand, expand], axis=0)
    out = _merge_out(os_, stats, x1m, b_w_in, expand, b_w_out, row(b_b_out[0]), row(post_ln_g[1]),
                     row(post_ln_b[1]))
    return out[None]
```
